```python
import jax, jax.numpy as jnp
from jax import lax
import numpy as np

D_MODEL = 1024
BATCH = 8
SEQ = 4096
DEPTH = 4

GRID_W = 64
CTX_LEN = 256
N_MIXERS = 2
N_FOURIER_LAYERS = (DEPTH + 1) // 2
N_HGRN_LAYERS = DEPTH // 2
FOURIER_GROUPS = 4
FOURIER_GROUP_DIM = D_MODEL // FOURIER_GROUPS
HGRN_HEADS = 8
HGRN_DK = D_MODEL // HGRN_HEADS
HGRN_DV = D_MODEL // HGRN_HEADS
HGRN_KD = HGRN_HEADS * HGRN_DK
HGRN_VD = HGRN_HEADS * HGRN_DV
HGRN_PROJ = 3 * HGRN_KD + 2 * HGRN_VD
HGRN_SPLITS = (HGRN_KD, HGRN_KD + HGRN_VD, 2 * HGRN_KD + HGRN_VD, 3 * HGRN_KD + HGRN_VD)
HGRN_CHUNK = 32
LB_FLOOR = 1e-30
D_FF = 2816
N_SUB = 3
N_MOD = 3 * N_SUB
HALF = 0.5
NORM_EPS = 1e-6

kernel_name = "hybrid_fourier_hgrn2_macaron_dit"


def rms_norm(x, g):
    xf = x.astype(jnp.float32)
    y = xf * lax.rsqrt(jnp.mean(xf * xf, axis=-1, keepdims=True) + NORM_EPS)
    return (y * g.astype(jnp.float32)).astype(x.dtype)


def pre_sub(s, mod, j, g):
    return rms_norm(s, g) * (1 + mod[:, :, 3 * j + 1]) + mod[:, :, 3 * j]


def post_sub(s, y, mod, j, g, w):
    return s + w * mod[:, :, 3 * j + 2] * rms_norm(y, g)


def swiglu(h, w_in, w_out):
    gate, up = jnp.split(h @ w_in, 2, axis=-1)
    return (jax.nn.silu(gate) * up) @ w_out


def fourier_grid(h, rows):
    B, L, D = h.shape
    hf = h.astype(jnp.float32).reshape(B, rows, GRID_W, FOURIER_GROUPS, FOURIER_GROUP_DIM)
    y = jnp.fft.fftn(hf, axes=(1, 2, 4), norm="ortho").real
    return y.reshape(B, L, D).astype(h.dtype)


def fourier_seq(h):
    B, L, D = h.shape
    hf = h.astype(jnp.float32).reshape(B, L, FOURIER_GROUPS, FOURIER_GROUP_DIM)
    y = jnp.fft.fftn(hf, axes=(1, 3), norm="ortho").real
    return y.reshape(B, L, D).astype(h.dtype)


def hgrn_lower_bound(lb_logits, j):
    p = jax.nn.softmax(lb_logits.astype(jnp.float32), axis=0)
    lb = jnp.cumsum(p, axis=0) - p[0]
    return lb[j]


def forget_gate(z, lb):
    lb = lb.reshape(HGRN_HEADS, 1, HGRN_DK)
    z = z.astype(jnp.float32)
    log_f = jnp.logaddexp(jnp.log(jnp.maximum(lb, LB_FLOOR)), jnp.log1p(-lb) + jax.nn.log_sigmoid(z))
    k = (1 - lb) * jax.nn.sigmoid(-z)
    return k, log_f


def hgrn2_inputs(h, w_in, lb_f, lb_b):
    B, L, _ = h.shape
    q, v, zf, zb, g = jnp.split(h @ w_in, HGRN_SPLITS, axis=-1)
    heads = lambda a: a.reshape(B, L, HGRN_HEADS, -1).transpose(0, 2, 1, 3).astype(jnp.float32)
    q = jax.nn.silu(heads(q))
    kf, lf = forget_gate(heads(zf), lb_f)
    kb, lbw = forget_gate(heads(zb), lb_b)
    return q, heads(v), kf, lf, kb, lbw, g


def gla_chunk_scan(q, k, v, log_f, s0):
    B, H, L, _ = q.shape
    n = L // HGRN_CHUNK
    to_chunks = lambda a: jnp.moveaxis(a.reshape(B, H, n, HGRN_CHUNK, a.shape[-1]), 2, 0)
    incl = jnp.tril(jnp.ones((HGRN_CHUNK, HGRN_CHUNK), bool))[:, :, None]

    def step(s, inp):
        qi, ki, vi, gi = inp
        b = jnp.cumsum(gi, axis=2)
        diff = b[:, :, :, None, :] - b[:, :, None, :, :]
        decay = jnp.where(incl, jnp.exp(jnp.where(incl, diff, 0.0)), 0.0)
        a = jnp.einsum('bhtk,bhsk,bhtsk->bhts', qi, ki, decay)
        o = (jnp.einsum('bhts,bhsv->bhtv', a, vi)
             + jnp.einsum('bhtk,bhkv->bhtv', qi * jnp.exp(b), s))
        b_last = b[:, :, -1:, :]
        s_new = (jnp.exp(b_last[:, :, 0, :])[..., None] * s
                 + jnp.einsum('bhsk,bhsv->bhkv', ki * jnp.exp(b_last - b), vi))
        return s_new, o

    s_fin, oc = lax.scan(step, s0, (to_chunks(q), to_chunks(k), to_chunks(v), to_chunks(log_f)))
    o = jnp.moveaxis(oc, 0, 2).reshape(B, H, L, -1)
    return o, s_fin


def bidir_scan(q, kf, lf, kb, lbw, v, s_f, s_b):
    flip = lambda a: jnp.flip(a, axis=2)
    o_f, s_f_out = gla_chunk_scan(q, kf, v, lf, s_f)
    o_b, s_b_out = gla_chunk_scan(flip(q), flip(kb), flip(v), flip(lbw), s_b)
    return o_f + flip(o_b), s_f_out, s_b_out


def hgrn2_readout(o, g, g_norm, w_out, dtype):
    B, H, L, _ = o.shape
    o = rms_norm(o, g_norm).transpose(0, 2, 1, 3).reshape(B, L, HGRN_VD).astype(dtype)
    return (o * jax.nn.silu(g)) @ w_out


def setup_inputs(seed: int = 0) -> dict:
    key = jax.random.key(seed)
    ks = jax.random.split(key, 16)
    d = D_MODEL
    nrm = lambda k, shape, scale: scale * jax.random.normal(k, shape, jnp.float32)
    return {
        "x": nrm(ks[0], (BATCH, SEQ, d), 1.0),
        "c": nrm(ks[1], (BATCH, d), 1.0),
        "ctx": nrm(ks[2], (BATCH, CTX_LEN, d), 1.0),
        "c_ctx": nrm(ks[3], (d,), 1.0),
        "ada_w": nrm(ks[4], (DEPTH, d, N_MOD * d), 0.5 * d ** -0.5),
        "ada_b": nrm(ks[5], (DEPTH, N_MOD * d), 0.02),
        "norm_pre": 1.0 + nrm(ks[6], (DEPTH, N_SUB, d), 0.05),
        "norm_post": 1.0 + nrm(ks[7], (DEPTH, N_SUB, d), 0.05),
        "ffn_w_in": nrm(ks[8], (DEPTH, 2, d, 2 * D_FF), d ** -0.5),
        "ffn_w_out": nrm(ks[9], (DEPTH, 2, D_FF, d), D_FF ** -0.5),
        "fourier_w_out": nrm(ks[10], (N_FOURIER_LAYERS, d, d), d ** -0.5),
        "hgrn_w_in": nrm(ks[11], (N_HGRN_LAYERS, d, HGRN_PROJ), d ** -0.5),
        "hgrn_lb_fwd": nrm(ks[12], (N_HGRN_LAYERS, HGRN_KD), 0.5),
        "hgrn_lb_bwd": nrm(ks[13], (N_HGRN_LAYERS, HGRN_KD), 0.5),
        "hgrn_norm": 1.0 + nrm(ks[14], (N_HGRN_LAYERS, HGRN_DV), 0.05),
        "hgrn_w_out": nrm(ks[15], (N_HGRN_LAYERS, HGRN_VD, d), HGRN_VD ** -0.5),
    }


def reference(x, c, ctx, c_ctx, ada_w, ada_b, norm_pre, norm_post, ffn_w_in, ffn_w_out,
              fourier_w_out, hgrn_w_in, hgrn_lb_fwd, hgrn_lb_bwd, hgrn_norm, hgrn_w_out):
    B, L, D = x.shape
    rows = L // GRID_W
    sc, sctx = jax.nn.silu(c), jax.nn.silu(c_ctx)
    for i in range(DEPTH):
        last = i == DEPTH - 1
        is_hgrn = i % N_MIXERS == 1
        jm = i // N_MIXERS
        ctx_in = is_hgrn or not last
        mx = (sc @ ada_w[i] + ada_b[i]).reshape(B, 1, N_MOD, D)
        mc = (sctx @ ada_w[i] + ada_b[i]).reshape(1, 1, N_MOD, D)

        def ffn_sub(s, mod, j, f):
            h = pre_sub(s, mod, j, norm_pre[i, j])
            return post_sub(s, swiglu(h, ffn_w_in[i, f], ffn_w_out[i, f]), mod, j, norm_post[i, j], HALF)

        x = ffn_sub(x, mx, 0, 0)
        if ctx_in:
            ctx = ffn_sub(ctx, mc, 0, 0)

        hx = pre_sub(x, mx, 1, norm_pre[i, 1])
        if not is_hgrn:
            yx = fourier_grid(hx, rows) @ fourier_w_out[jm]
            x = post_sub(x, yx, mx, 1, norm_post[i, 1], 1)
            if not last:
                hc = pre_sub(ctx, mc, 1, norm_pre[i, 1])
                yc = fourier_seq(hc) @ fourier_w_out[jm]
                ctx = post_sub(ctx, yc, mc, 1, norm_post[i, 1], 1)
        else:
            lb_f = hgrn_lower_bound(hgrn_lb_fwd, jm)
            lb_b = hgrn_lower_bound(hgrn_lb_bwd, jm)
            hc = pre_sub(ctx, mc, 1, norm_pre[i, 1])
            qc, vc, kfc, lfc, kbc, lbc, gc = hgrn2_inputs(hc, hgrn_w_in[jm], lb_f, lb_b)
            zero = jnp.zeros((B, HGRN_HEADS, HGRN_DK, HGRN_DV), jnp.float32)
            oc, s_f, s_b = bidir_scan(qc, kfc, lfc, kbc, lbc, vc, zero, zero)
            qx, vx, kfx, lfx, kbx, lbx, gx = hgrn2_inputs(hx, hgrn_w_in[jm], lb_f, lb_b)
            ox, _, _ = bidir_scan(qx, kfx, lfx, kbx, lbx, vx, s_f, s_b)
            yx = hgrn2_readout(ox, gx, hgrn_norm[jm], hgrn_w_out[jm], x.dtype)
            x = post_sub(x, yx, mx, 1, norm_post[i, 1], 1)
            if not last:
                yc = hgrn2_readout(oc, gc, hgrn_norm[jm], hgrn_w_out[jm], ctx.dtype)
                ctx = post_sub(ctx, yc, mc, 1, norm_post[i, 1], 1)

        x = ffn_sub(x, mx, 2, 1)
        if not last:
            ctx = ffn_sub(ctx, mc, 2, 1)
    return x
```

```python
import functools

import jax
import jax.numpy as jnp
import numpy as np
from jax import lax
from jax.experimental import pallas as pl
from jax.experimental.pallas import tpu as pltpu

F32 = jnp.float32
BF16 = jnp.bfloat16

GRID_W = 64
FOURIER_GROUPS = 4
HGRN_HEADS = 8
N_MOD = 9
HALF = 0.5
NORM_EPS = 1e-6
LB_FLOOR = 1e-30

VMEM_LIMIT_V7X = 56 * 1024 * 1024
SUBLANES = 8
SCAN_TILE = 128
MOD_ROWS = 16


def _params(*sem):
    return pltpu.CompilerParams(dimension_semantics=sem, vmem_limit_bytes=VMEM_LIMIT_V7X)


def _const_spec(shape):
    n = len(shape)
    return pl.BlockSpec(shape, lambda *_: (0,) * n, pipeline_mode=pl.Buffered(1))


def _rms(x, g):
    ms = jnp.mean(x * x, axis=-1, keepdims=True)
    return x * lax.rsqrt(ms + NORM_EPS) * g


def _pre(x, mod_ref, g_ref):
    return _rms(x, g_ref[...]) * (1.0 + mod_ref[1:2, :]) + mod_ref[0:1, :]


def _post(x, y, mod_ref, g_ref, w):
    return x + w * mod_ref[2:3, :] * _rms(y, g_ref[...])


def _silu(x):
    return x * jax.nn.sigmoid(x)


def _dot(a, b):
    return jnp.dot(a, b, preferred_element_type=F32)


def _dot_nt(a, b):
    return lax.dot_general(a, b, (((1,), (1,)), ((), ())), preferred_element_type=F32)


def _dot_tn(a, b):
    return lax.dot_general(a, b, (((0,), (0,)), ((), ())), preferred_element_type=F32)


def _adaln_kernel(c_ref, w_ref, b_ref, o_ref):
    sc = _silu(c_ref[...]).astype(BF16)
    o_ref[...] = _dot(sc, w_ref[...].astype(BF16)) + b_ref[...]


def _adaln(c_rows, ada_w, ada_b):
    depth, d, n = ada_w.shape
    tn = 1024
    return pl.pallas_call(
        _adaln_kernel,
        grid=(depth, n // tn),
        in_specs=[
            pl.BlockSpec((MOD_ROWS, d), lambda i, j: (0, 0)),
            pl.BlockSpec((None, d, tn), lambda i, j: (i, 0, j)),
            pl.BlockSpec((None, 1, tn), lambda i, j: (i, 0, j)),
        ],
        out_specs=pl.BlockSpec((None, MOD_ROWS, tn), lambda i, j: (i, 0, j)),
        out_shape=jax.ShapeDtypeStruct((depth, MOD_ROWS, n), F32),
        compiler_params=_params("parallel", "parallel"),
        name="adaln",
    )(c_rows, ada_w, ada_b.reshape(depth, 1, n))


def _ffn_kernel(x_ref, mod_ref, gpre_ref, gpost_ref, win_ref, wout_ref, o_ref, *, d_ff, n_chunks):
    x = x_ref[...]
    hb = _pre(x, mod_ref, gpre_ref).astype(BF16)
    fc = d_ff // n_chunks
    acc = None
    for ci in range(n_chunks):
        gate = _dot(hb, win_ref[:, ci * fc:(ci + 1) * fc])
        up = _dot(hb, win_ref[:, d_ff + ci * fc:d_ff + (ci + 1) * fc])
        a = (_silu(gate) * up).astype(BF16)
        y = _dot(a, wout_ref[ci * fc:(ci + 1) * fc, :])
        acc = y if acc is None else acc + y
    o_ref[...] = _post(x, acc, mod_ref, gpost_ref, HALF)


def _token_tile(n_tokens, want):
    return want if n_tokens % want == 0 else n_tokens


def _ffn(s, mod, gpre, gpost, w_in, w_out):
    b, l, d = s.shape
    d_ff = w_out.shape[0]
    tm = _token_tile(l, 512)
    return pl.pallas_call(
        functools.partial(_ffn_kernel, d_ff=d_ff, n_chunks=2),
        grid=(b, l // tm),
        in_specs=[
            pl.BlockSpec((None, tm, d), lambda i, j: (i, j, 0)),
            pl.BlockSpec((None, 3, d), lambda i, j: (i, 0, 0)),
            _const_spec((1, d)),
            _const_spec((1, d)),
            _const_spec(w_in.shape),
            _const_spec(w_out.shape),
        ],
        out_specs=pl.BlockSpec((None, tm, d), lambda i, j: (i, j, 0)),
        out_shape=jax.ShapeDtypeStruct(s.shape, F32),
        compiler_params=_params("parallel", "parallel"),
        name="ffn",
    )(s, mod, gpre, gpost, w_in, w_out)


def _dft_cos_sin(n):
    k = np.arange(n)
    ang = 2.0 * np.pi * ((k[:, None] * k[None, :]) % n) / n
    s = 1.0 / np.sqrt(n)
    return np.cos(ang) * s, np.sin(ang) * s


def _fourier_consts(gd):
    c, s = _dft_cos_sin(gd)
    chan = np.concatenate([c, s], axis=1)
    c, s = _dft_cos_sin(GRID_W)
    col = np.block([[c, -s], [s, c]])
    return jnp.asarray(chan, BF16), jnp.asarray(col, BF16)


def _channel_dft(hb, chan_ref, gd):
    us, vs = [], []
    for g in range(FOURIER_GROUPS):
        uv = _dot(hb[:, g * gd:(g + 1) * gd], chan_ref[...])
        us.append(uv[:, :gd])
        vs.append(uv[:, gd:])
    return (jnp.concatenate(us, axis=1).astype(BF16), jnp.concatenate(vs, axis=1).astype(BF16))


def _fourier_cols_kernel(x_ref, mod_ref, gpre_ref, chan_ref, col_ref, a_ref, b_ref, *, rows_per_step, gd):
    hb = _pre(x_ref[...], mod_ref, gpre_ref).astype(BF16)
    u, v = _channel_dft(hb, chan_ref, gd)
    for r in range(rows_per_step):
        sl = slice(r * GRID_W, (r + 1) * GRID_W)
        ab = _dot(col_ref[...], jnp.concatenate([u[sl], v[sl]], axis=0))
        a_ref[sl, :] = ab[:GRID_W].astype(BF16)
        b_ref[sl, :] = ab[GRID_W:].astype(BF16)


def _fourier_rows_kernel(a_ref, b_ref, x_ref, mod_ref, gpost_ref, row_ref, w_ref, o_ref, *, cols_per_step, d):
    ab = jnp.concatenate([a_ref[...], b_ref[...]], axis=0)
    y = _dot(row_ref[...], ab)
    lanes = [slice(c * d, (c + 1) * d) for c in range(cols_per_step)]
    yt = jnp.concatenate([y[:, sl] for sl in lanes], axis=0).astype(BF16)
    xt = jnp.concatenate([x_ref[:, sl] for sl in lanes], axis=0)
    res = _post(xt, _dot(yt, w_ref[...]), mod_ref, gpost_ref, 1.0)
    rows = y.shape[0]
    for c, sl in enumerate(lanes):
        o_ref[:, sl] = res[c * rows:(c + 1) * rows, :]


def _fourier_latent(x, mod, gpre, gpost, w_out):
    b, l, d = x.shape
    rows = l // GRID_W
    gd = d // FOURIER_GROUPS
    chan, col = _fourier_consts(gd)
    rps = 4 if rows % 4 == 0 else 1
    tm = rps * GRID_W
    a, bq = pl.pallas_call(
        functools.partial(_fourier_cols_kernel, rows_per_step=rps, gd=gd),
        grid=(b, l // tm),
        in_specs=[
            pl.BlockSpec((None, tm, d), lambda i, j: (i, j, 0)),
            pl.BlockSpec((None, 3, d), lambda i, j: (i, 0, 0)),
            _const_spec((1, d)),
            _const_spec(chan.shape),
            _const_spec(col.shape),
        ],
        out_specs=[pl.BlockSpec((None, tm, d), lambda i, j: (i, j, 0))] * 2,
        out_shape=[jax.ShapeDtypeStruct((b, l, d), BF16)] * 2,
        compiler_params=_params("parallel", "parallel"),
        name="fourier_cols",
    )(x, mod, gpre, chan, col)

    c, s = _dft_cos_sin(rows)
    rowm = jnp.asarray(np.concatenate([c, -s], axis=1), BF16)
    cps = 8
    wide = GRID_W * d
    view = lambda t: t.reshape(b, rows, wide)
    out = pl.pallas_call(
        functools.partial(_fourier_rows_kernel, cols_per_step=cps, d=d),
        grid=(b, GRID_W // cps),
        in_specs=[
            pl.BlockSpec((None, rows, cps * d), lambda i, j: (i, 0, j)),
            pl.BlockSpec((None, rows, cps * d), lambda i, j: (i, 0, j)),
            pl.BlockSpec((None, rows, cps * d), lambda i, j: (i, 0, j)),
            pl.BlockSpec((None, 3, d), lambda i, j: (i, 0, 0)),
            _const_spec((1, d)),
            _const_spec(rowm.shape),
            _const_spec(w_out.shape),
        ],
        out_specs=pl.BlockSpec((None, rows, cps * d), lambda i, j: (i, 0, j)),
        out_shape=jax.ShapeDtypeStruct((b, rows, wide), F32),
        compiler_params=_params("parallel", "parallel"),
        name="fourier_rows",
    )(view(a), view(bq), view(x), mod, gpost, rowm, w_out)
    return out.reshape(b, l, d)


def _fourier_ctx_kernel(x_ref, mod_ref, gpre_ref, gpost_ref, chan_ref, seq_ref, w_ref, o_ref, *, gd):
    x = x_ref[...]
    hb = _pre(x, mod_ref, gpre_ref).astype(BF16)
    u, v = _channel_dft(hb, chan_ref, gd)
    y = _dot(seq_ref[...], jnp.concatenate([u, v], axis=0)).astype(BF16)
    o_ref[...] = _post(x, _dot(y, w_ref[...]), mod_ref, gpost_ref, 1.0)


def _fourier_ctx(x, mod, gpre, gpost, w_out):
    b, l, d = x.shape
    gd = d // FOURIER_GROUPS
    chan, _ = _fourier_consts(gd)
    c, s = _dft_cos_sin(l)
    seqm = jnp.asarray(np.concatenate([c, -s], axis=1), BF16)
    return pl.pallas_call(
        functools.partial(_fourier_ctx_kernel, gd=gd),
        grid=(b,),
        in_specs=[
            pl.BlockSpec((None, l, d), lambda i: (i, 0, 0)),
            pl.BlockSpec((None, 3, d), lambda i: (0, 0, 0)),
            _const_spec((1, d)),
            _const_spec((1, d)),
            _const_spec(chan.shape),
            _const_spec(seqm.shape),
            _const_spec(w_out.shape),
        ],
        out_specs=pl.BlockSpec((None, l, d), lambda i: (i, 0, 0)),
        out_shape=jax.ShapeDtypeStruct(x.shape, F32),
        compiler_params=_params("parallel"),
        name="fourier_ctx",
    )(x, mod, gpre, gpost, chan, seqm, w_out)


def _lower_bound(logits_ref, j):
    lg = logits_ref[...]
    e = jnp.exp(lg - jnp.max(lg, axis=0, keepdims=True))
    p = e / jnp.sum(e, axis=0, keepdims=True)
    lb = jnp.zeros_like(p[0:1])
    for i in range(1, j + 1):
        lb = lb + p[i:i + 1]
    return lb


def _forget_gate(z, lb):
    e = jnp.exp(-jnp.abs(z))
    inv = 1.0 / (1.0 + e)
    pos = z >= 0.0
    sig = jnp.where(pos, inv, e * inv)
    nsig = jnp.where(pos, e * inv, inv)
    f = jnp.maximum(lb, LB_FLOOR) + (1.0 - lb) * sig
    return (1.0 - lb) * nsig, jnp.log(f)


def _hgrn_in_kernel(x_ref, mod_ref, gpre_ref, w_ref, lbf_ref, lbb_ref,
                    q_ref, v_ref, kf_ref, lf_ref, kb_ref, lb_ref, sg_ref, *, layer_j, kd, hd):
    hb = _pre(x_ref[...], mod_ref, gpre_ref).astype(BF16)
    col = lambda n: _dot(hb, w_ref[:, n * kd:(n + 1) * kd])
    q = _silu(col(0))
    v = col(1)
    kf, lf = _forget_gate(col(2), _lower_bound(lbf_ref, layer_j))
    kb, lbw = _forget_gate(col(3), _lower_bound(lbb_ref, layer_j))
    sg_ref[...] = _silu(col(4)).astype(BF16)
    for h in range(HGRN_HEADS):
        sl = slice(h * hd, (h + 1) * hd)
        q_ref[h] = q[:, sl].astype(BF16)
        v_ref[h] = v[:, sl].astype(BF16)
        kf_ref[h] = kf[:, sl].astype(BF16)
        kb_ref[h] = kb[:, sl].astype(BF16)
        lf_ref[h] = lf[:, sl]
        lb_ref[h] = lbw[:, sl]


def _hgrn_inputs(x, mod, gpre, w_in, lb_fwd, lb_bwd, layer_j):
    b, l, d = x.shape
    kd = d
    hd = kd // HGRN_HEADS
    tm = _token_tile(l, 512)
    x_spec = pl.BlockSpec((None, tm, d), lambda i, j: (i, j, 0))
    head_spec = pl.BlockSpec((None, HGRN_HEADS, tm, hd), lambda i, j: (i, 0, j, 0))
    heads = lambda dt: jax.ShapeDtypeStruct((b, HGRN_HEADS, l, hd), dt)
    return pl.pallas_call(
        functools.partial(_hgrn_in_kernel, layer_j=layer_j, kd=kd, hd=hd),
        grid=(b, l // tm),
        in_specs=[
            x_spec,
            pl.BlockSpec((None, 3, d), lambda i, j: (i, 0, 0)),
            _const_spec((1, d)),
            _const_spec(w_in.shape),
            _const_spec(lb_fwd.shape),
            _const_spec(lb_bwd.shape),
        ],
        out_specs=[head_spec] * 6 + [x_spec],
        out_shape=[heads(BF16), heads(BF16), heads(BF16), heads(F32), heads(BF16), heads(F32),
                   jax.ShapeDtypeStruct((b, l, d), BF16)],
        compiler_params=_params("parallel", "parallel"),
        name="hgrn_in",
    )(x, mod, gpre, w_in, lb_fwd, lb_bwd)


def _split3(x):
    hi = x.astype(BF16)
    r = x - hi.astype(F32)
    mid = r.astype(BF16)
    lo = (r - mid.astype(F32)).astype(BF16)
    return hi, mid, lo


def _scan_kernel(q_ref, k_ref, v_ref, g_ref, s0_ref, o_ref, sfin_ref, state_ref, *, reverse, n_tiles):
    t = q_ref.shape[0]
    step = pl.program_id(2)

    @pl.when(step == 0)
    def _():
        state_ref[...] = s0_ref[...]

    q = q_ref[...].astype(F32)
    k = k_ref[...].astype(F32)
    vb = v_ref[...]
    g = g_ref[...]

    ri = lax.broadcasted_iota(jnp.int32, (t, t), 0)
    ci = lax.broadcasted_iota(jnp.int32, (t, t), 1)
    tri = ((ci >= ri) if reverse else (ci <= ri)).astype(BF16)
    b = sum(_dot(tri, part) for part in _split3(g))
    end = 0 if reverse else t - 1
    b_end = b[end:end + 1, :]

    st = state_ref[...]
    o = _dot_nt((q * jnp.exp(b)).astype(BF16), st.astype(BF16))
    k_end = (k * jnp.exp(b_end - b)).astype(BF16)
    state_ref[...] = st * jnp.exp(b_end) + _dot_tn(vb, k_end)

    a = jnp.zeros((t, t), F32)
    m = t // 2
    while m >= SUBLANES:
        first = m if reverse else m - 1
        bref = jnp.concatenate(
            [jnp.broadcast_to(b[p + first:p + first + 1, :], (2 * m, b.shape[1])) for p in range(0, t, 2 * m)],
            axis=0)
        e = jnp.exp(-jnp.abs(b - bref))
        s_m = _dot_nt((q * e).astype(BF16), (k * e).astype(BF16))
        same = (ri // (2 * m)) == (ci // (2 * m))
        late, early = (ci, ri) if reverse else (ri, ci)
        keep = same & ((late // m) % 2 == 1) & ((early // m) % 2 == 0)
        a = a + jnp.where(keep, s_m, 0.0)
        m //= 2
    o = o + _dot(a.astype(BF16), vb)

    nb = t // SUBLANES
    blk = lambda z: z.reshape(nb, SUBLANES, z.shape[-1])
    q3, k3, v3, b3 = blk(q), blk(k), blk(vb.astype(F32)), blk(b)
    rid = lax.broadcasted_iota(jnp.int32, q3.shape, 1)
    od = jnp.zeros(q3.shape, F32)
    for s in range(SUBLANES):
        keep = (rid <= s) if reverse else (rid >= s)
        dec = jnp.where(keep, jnp.exp(jnp.minimum(b3 - b3[:, s:s + 1, :], 0.0)), 0.0)
        w = jnp.sum(q3 * k3[:, s:s + 1, :] * dec, axis=-1, keepdims=True)
        od = od + w * v3[:, s:s + 1, :]
    o_ref[...] = o + od.reshape(t, od.shape[-1])

    @pl.when(step == n_tiles - 1)
    def _():
        sfin_ref[...] = state_ref[...]


def _scan(q, k, v, g, s0, reverse):
    b, h, l, hd = q.shape
    t = _token_tile(l, SCAN_TILE)
    n_tiles = l // t
    tile = (lambda j: n_tiles - 1 - j) if reverse else (lambda j: j)
    tok_spec = pl.BlockSpec((None, None, t, hd), lambda i, hh, j: (i, hh, tile(j), 0))
    st_spec = pl.BlockSpec((None, None, hd, hd), lambda i, hh, j: (i, hh, 0, 0))
    return pl.pallas_call(
        functools.partial(_scan_kernel, reverse=reverse, n_tiles=n_tiles),
        grid=(b, h, n_tiles),
        in_specs=[tok_spec, tok_spec, tok_spec, tok_spec, st_spec],
        out_specs=[tok_spec, st_spec],
        out_shape=[jax.ShapeDtypeStruct((b, h, l, hd), F32), jax.ShapeDtypeStruct((b, h, hd, hd), F32)],
        scratch_shapes=[pltpu.VMEM((hd, hd), F32)],
        compiler_params=_params("parallel", "parallel", "arbitrary"),
        name="scan_bwd" if reverse else "scan_fwd",
    )(q, k, v, g, s0)


def _hgrn_out_kernel(of_ref, ob_ref, sg_ref, x_ref, mod_ref, gpost_ref, gn_ref, w_ref, o_ref):
    heads = []
    for h in range(HGRN_HEADS):
        heads.append(_rms(of_ref[h] + ob_ref[h], gn_ref[...]))
    o = jnp.concatenate(heads, axis=1)
    y = _dot((o * sg_ref[...].astype(F32)).astype(BF16), w_ref[...])
    o_ref[...] = _post(x_ref[...], y, mod_ref, gpost_ref, 1.0)


def _hgrn_readout(o_f, o_b, sg, x, mod, gpost, g_norm, w_out):
    b, l, d = x.shape
    hd = d // HGRN_HEADS
    tm = _token_tile(l, 512)
    x_spec = pl.BlockSpec((None, tm, d), lambda i, j: (i, j, 0))
    head_spec = pl.BlockSpec((None, HGRN_HEADS, tm, hd), lambda i, j: (i, 0, j, 0))
    return pl.pallas_call(
        _hgrn_out_kernel,
        grid=(b, l // tm),
        in_specs=[
            head_spec, head_spec, x_spec, x_spec,
            pl.BlockSpec((None, 3, d), lambda i, j: (i, 0, 0)),
            _const_spec((1, d)),
            _const_spec((1, hd)),
            _const_spec(w_out.shape),
        ],
        out_specs=x_spec,
        out_shape=jax.ShapeDtypeStruct(x.shape, F32),
        compiler_params=_params("parallel", "parallel"),
        name="hgrn_out",
    )(o_f, o_b, sg, x, mod, gpost, g_norm, w_out)


def kernel(x, c, ctx, c_ctx, ada_w, ada_b, norm_pre, norm_post, ffn_w_in, ffn_w_out, fourier_w_out,
           hgrn_w_in, hgrn_lb_fwd, hgrn_lb_bwd, hgrn_norm, hgrn_w_out):
    bsz, _, d = x.shape
    depth = ada_w.shape[0]
    assert bsz + 1 <= MOD_ROWS

    c_rows = jnp.concatenate([c, c_ctx[None, :], jnp.zeros((MOD_ROWS - bsz - 1, d), F32)], axis=0)
    mod = _adaln(c_rows, ada_w, ada_b).reshape(depth, MOD_ROWS, N_MOD, d)

    ffn_w_in = ffn_w_in.astype(BF16)
    ffn_w_out = ffn_w_out.astype(BF16)
    fourier_w_out = fourier_w_out.astype(BF16)
    hgrn_w_in = hgrn_w_in.astype(BF16)
    hgrn_w_out = hgrn_w_out.astype(BF16)

    for i in range(depth):
        last = i == depth - 1
        is_hgrn = i % 2 == 1
        jm = i // 2
        mx = lambda j: mod[i, :bsz, 3 * j:3 * j + 3]
        mc = lambda j: mod[i, bsz:bsz + 1, 3 * j:3 * j + 3]
        gpre = lambda j: norm_pre[i, j][None, :]
        gpost = lambda j: norm_post[i, j][None, :]
        ctx_mod = lambda j: jnp.broadcast_to(mc(j), (bsz, 3, d))

        x = _ffn(x, mx(0), gpre(0), gpost(0), ffn_w_in[i, 0], ffn_w_out[i, 0])
        if is_hgrn or not last:
            ctx = _ffn(ctx, ctx_mod(0), gpre(0), gpost(0), ffn_w_in[i, 0], ffn_w_out[i, 0])

        if not is_hgrn:
            x = _fourier_latent(x, mx(1), gpre(1), gpost(1), fourier_w_out[jm])
            if not last:
                ctx = _fourier_ctx(ctx, mc(1), gpre(1), gpost(1), fourier_w_out[jm])
        else:
            hd = d // HGRN_HEADS
            zero = jnp.zeros((bsz, HGRN_HEADS, hd, hd), F32)
            qc, vc, kfc, lfc, kbc, lbc, sgc = _hgrn_inputs(
                ctx, ctx_mod(1), gpre(1), hgrn_w_in[jm], hgrn_lb_fwd, hgrn_lb_bwd, jm)
            ocf, s_f = _scan(qc, kfc, vc, lfc, zero, False)
            ocb, s_b = _scan(qc, kbc, vc, lbc, zero, True)
            qx, vx, kfx, lfx, kbx, lbx, sgx = _hgrn_inputs(
                x, mx(1), gpre(1), hgrn_w_in[jm], hgrn_lb_fwd, hgrn_lb_bwd, jm)
            oxf, _ = _scan(qx, kfx, vx, lfx, s_f, False)
            oxb, _ = _scan(qx, kbx, vx, lbx, s_b, True)
            gn = hgrn_norm[jm][None, :]
            x = _hgrn_readout(oxf, oxb, sgx, x, mx(1), gpost(1), gn, hgrn_w_out[jm])
            if not last:
                ctx = _hgrn_readout(ocf, ocb, sgc, ctx, ctx_mod(1), gpost(1), gn, hgrn_w_out[jm])

        x = _ffn(x, mx(2), gpre(2), gpost(2), ffn_w_in[i, 1], ffn_w_out[i, 1])
        if not last:
            ctx = _ffn(ctx, ctx_mod(2), gpre(2), gpost(2), ffn_w_in[i, 1], ffn_w_out[i, 1])
    return x
```

```python
import functools

import jax
import jax.numpy as jnp
import numpy as np
from jax import lax
from jax.experimental import pallas as pl
from jax.experimental.pallas import tpu as pltpu

F32 = jnp.float32
BF16 = jnp.bfloat16

GRID_W = 64
FOURIER_GROUPS = 4
HGRN_HEADS = 8
N_MOD = 9
HALF = 0.5
NORM_EPS = 1e-6
LB_FLOOR = 1e-30

VMEM_LIMIT_V7X = 56 * 1024 * 1024
SUBLANES = 8
SCAN_TILE = 128
SCAN_HEADS = 4
MOD_ROWS = 16


def _params(*sem):
    return pltpu.CompilerParams(dimension_semantics=sem, vmem_limit_bytes=VMEM_LIMIT_V7X)


def _const_spec(shape):
    n = len(shape)
    return pl.BlockSpec(shape, lambda *_: (0,) * n, pipeline_mode=pl.Buffered(1))


def _rms(x, g):
    ms = jnp.mean(x * x, axis=-1, keepdims=True)
    return x * lax.rsqrt(ms + NORM_EPS) * g


def _pre(x, mod_ref, g_ref):
    return _rms(x, g_ref[...]) * (1.0 + mod_ref[1:2, :]) + mod_ref[0:1, :]


def _post(x, y, mod_ref, g_ref, w):
    return x + w * mod_ref[2:3, :] * _rms(y, g_ref[...])


def _silu(x):
    return x * jax.nn.sigmoid(x)


def _dot(a, b):
    return jnp.dot(a, b, preferred_element_type=F32)


def _dot_nt(a, b):
    return lax.dot_general(a, b, (((1,), (1,)), ((), ())), preferred_element_type=F32)


def _dot_tn(a, b):
    return lax.dot_general(a, b, (((0,), (0,)), ((), ())), preferred_element_type=F32)


def _adaln_kernel(c_ref, w_ref, b_ref, o_ref):
    sc = _silu(c_ref[...]).astype(BF16)
    o_ref[...] = _dot(sc, w_ref[...].astype(BF16)) + b_ref[...]


def _adaln(c_rows, ada_w, ada_b):
    depth, d, n = ada_w.shape
    tn = 1024
    return pl.pallas_call(
        _adaln_kernel,
        grid=(depth, n // tn),
        in_specs=[
            pl.BlockSpec((MOD_ROWS, d), lambda i, j: (0, 0)),
            pl.BlockSpec((None, d, tn), lambda i, j: (i, 0, j)),
            pl.BlockSpec((None, 1, tn), lambda i, j: (i, 0, j)),
        ],
        out_specs=pl.BlockSpec((None, MOD_ROWS, tn), lambda i, j: (i, 0, j)),
        out_shape=jax.ShapeDtypeStruct((depth, MOD_ROWS, n), F32),
        compiler_params=_params("parallel", "parallel"),
        name="adaln",
    )(c_rows, ada_w, ada_b.reshape(depth, 1, n))


def _ffn_kernel(x_ref, mod_ref, gpre_ref, gpost_ref, win_ref, wout_ref, o_ref, *, d_ff, n_chunks):
    x = x_ref[...]
    hb = _pre(x, mod_ref, gpre_ref).astype(BF16)
    fc = d_ff // n_chunks
    acc = None
    for ci in range(n_chunks):
        gate = _dot(hb, win_ref[:, ci * fc:(ci + 1) * fc])
        up = _dot(hb, win_ref[:, d_ff + ci * fc:d_ff + (ci + 1) * fc])
        a = (_silu(gate) * up).astype(BF16)
        y = _dot(a, wout_ref[ci * fc:(ci + 1) * fc, :])
        acc = y if acc is None else acc + y
    o_ref[...] = _post(x, acc, mod_ref, gpost_ref, HALF)


def _token_tile(n_tokens, want):
    return want if n_tokens % want == 0 else n_tokens


def _ffn(s, mod, gpre, gpost, w_in, w_out):
    b, l, d = s.shape
    d_ff = w_out.shape[0]
    tm = _token_tile(l, 512)
    return pl.pallas_call(
        functools.partial(_ffn_kernel, d_ff=d_ff, n_chunks=2),
        grid=(b, l // tm),
        in_specs=[
            pl.BlockSpec((None, tm, d), lambda i, j: (i, j, 0)),
            pl.BlockSpec((None, 3, d), lambda i, j: (i, 0, 0)),
            _const_spec((1, d)),
            _const_spec((1, d)),
            _const_spec(w_in.shape),
            _const_spec(w_out.shape),
        ],
        out_specs=pl.BlockSpec((None, tm, d), lambda i, j: (i, j, 0)),
        out_shape=jax.ShapeDtypeStruct(s.shape, F32),
        compiler_params=_params("parallel", "parallel"),
        name="ffn",
    )(s, mod, gpre, gpost, w_in, w_out)


def _dft_cos_sin(n):
    k = np.arange(n)
    ang = 2.0 * np.pi * ((k[:, None] * k[None, :]) % n) / n
    s = 1.0 / np.sqrt(n)
    return np.cos(ang) * s, np.sin(ang) * s


def _fourier_consts(gd):
    c, s = _dft_cos_sin(gd)
    chan = np.concatenate([c, s], axis=1)
    c, s = _dft_cos_sin(GRID_W)
    col = np.block([[c, -s], [s, c]])
    return jnp.asarray(chan, BF16), jnp.asarray(col, BF16)


def _channel_dft(hb, chan_ref, gd):
    us, vs = [], []
    for g in range(FOURIER_GROUPS):
        uv = _dot(hb[:, g * gd:(g + 1) * gd], chan_ref[...])
        us.append(uv[:, :gd])
        vs.append(uv[:, gd:])
    return (jnp.concatenate(us, axis=1).astype(BF16), jnp.concatenate(vs, axis=1).astype(BF16))


def _fourier_cols_kernel(x_ref, mod_ref, gpre_ref, chan_ref, col_ref, a_ref, b_ref, *, rows_per_step, gd):
    hb = _pre(x_ref[...], mod_ref, gpre_ref).astype(BF16)
    u, v = _channel_dft(hb, chan_ref, gd)
    for r in range(rows_per_step):
        sl = slice(r * GRID_W, (r + 1) * GRID_W)
        ab = _dot(col_ref[...], jnp.concatenate([u[sl], v[sl]], axis=0))
        a_ref[sl, :] = ab[:GRID_W].astype(BF16)
        b_ref[sl, :] = ab[GRID_W:].astype(BF16)


def _fourier_rows_kernel(a_ref, b_ref, x_ref, mod_ref, gpost_ref, row_ref, w_ref, o_ref, *, cols_per_step, d):
    ab = jnp.concatenate([a_ref[...], b_ref[...]], axis=0)
    y = _dot(row_ref[...], ab)
    lanes = [slice(c * d, (c + 1) * d) for c in range(cols_per_step)]
    yt = jnp.concatenate([y[:, sl] for sl in lanes], axis=0).astype(BF16)
    xt = jnp.concatenate([x_ref[:, sl] for sl in lanes], axis=0)
    res = _post(xt, _dot(yt, w_ref[...]), mod_ref, gpost_ref, 1.0)
    rows = y.shape[0]
    for c, sl in enumerate(lanes):
        o_ref[:, sl] = res[c * rows:(c + 1) * rows, :]


def _fourier_latent(x, mod, gpre, gpost, w_out):
    b, l, d = x.shape
    rows = l // GRID_W
    gd = d // FOURIER_GROUPS
    chan, col = _fourier_consts(gd)
    rps = 4 if rows % 4 == 0 else 1
    tm = rps * GRID_W
    a, bq = pl.pallas_call(
        functools.partial(_fourier_cols_kernel, rows_per_step=rps, gd=gd),
        grid=(b, l // tm),
        in_specs=[
            pl.BlockSpec((None, tm, d), lambda i, j: (i, j, 0)),
            pl.BlockSpec((None, 3, d), lambda i, j: (i, 0, 0)),
            _const_spec((1, d)),
            _const_spec(chan.shape),
            _const_spec(col.shape),
        ],
        out_specs=[pl.BlockSpec((None, tm, d), lambda i, j: (i, j, 0))] * 2,
        out_shape=[jax.ShapeDtypeStruct((b, l, d), BF16)] * 2,
        compiler_params=_params("parallel", "parallel"),
        name="fourier_cols",
    )(x, mod, gpre, chan, col)

    c, s = _dft_cos_sin(rows)
    rowm = jnp.asarray(np.concatenate([c, -s], axis=1), BF16)
    cps = 8
    wide = GRID_W * d
    view = lambda t: t.reshape(b, rows, wide)
    out = pl.pallas_call(
        functools.partial(_fourier_rows_kernel, cols_per_step=cps, d=d),
        grid=(b, GRID_W // cps),
        in_specs=[
            pl.BlockSpec((None, rows, cps * d), lambda i, j: (i, 0, j)),
            pl.BlockSpec((None, rows, cps * d), lambda i, j: (i, 0, j)),
            pl.BlockSpec((None, rows, cps * d), lambda i, j: (i, 0, j)),
            pl.BlockSpec((None, 3, d), lambda i, j: (i, 0, 0)),
            _const_spec((1, d)),
            _const_spec(rowm.shape),
            _const_spec(w_out.shape),
        ],
        out_specs=pl.BlockSpec((None, rows, cps * d), lambda i, j: (i, 0, j)),
        out_shape=jax.ShapeDtypeStruct((b, rows, wide), F32),
        compiler_params=_params("parallel", "parallel"),
        name="fourier_rows",
    )(view(a), view(bq), view(x), mod, gpost, rowm, w_out)
    return out.reshape(b, l, d)


def _fourier_ctx_kernel(x_ref, mod_ref, gpre_ref, gpost_ref, chan_ref, seq_ref, w_ref, o_ref, *, gd):
    x = x_ref[...]
    hb = _pre(x, mod_ref, gpre_ref).astype(BF16)
    u, v = _channel_dft(hb, chan_ref, gd)
    y = _dot(seq_ref[...], jnp.concatenate([u, v], axis=0)).astype(BF16)
    o_ref[...] = _post(x, _dot(y, w_ref[...]), mod_ref, gpost_ref, 1.0)


def _fourier_ctx(x, mod, gpre, gpost, w_out):
    b, l, d = x.shape
    gd = d // FOURIER_GROUPS
    chan, _ = _fourier_consts(gd)
    c, s = _dft_cos_sin(l)
    seqm = jnp.asarray(np.concatenate([c, -s], axis=1), BF16)
    return pl.pallas_call(
        functools.partial(_fourier_ctx_kernel, gd=gd),
        grid=(b,),
        in_specs=[
            pl.BlockSpec((None, l, d), lambda i: (i, 0, 0)),
            pl.BlockSpec((None, 3, d), lambda i: (0, 0, 0)),
            _const_spec((1, d)),
            _const_spec((1, d)),
            _const_spec(chan.shape),
            _const_spec(seqm.shape),
            _const_spec(w_out.shape),
        ],
        out_specs=pl.BlockSpec((None, l, d), lambda i: (i, 0, 0)),
        out_shape=jax.ShapeDtypeStruct(x.shape, F32),
        compiler_params=_params("parallel"),
        name="fourier_ctx",
    )(x, mod, gpre, gpost, chan, seqm, w_out)


def _lower_bound(logits_ref, j):
    lg = logits_ref[...]
    e = jnp.exp(lg - jnp.max(lg, axis=0, keepdims=True))
    p = e / jnp.sum(e, axis=0, keepdims=True)
    lb = jnp.zeros_like(p[0:1])
    for i in range(1, j + 1):
        lb = lb + p[i:i + 1]
    return lb


def _forget_gate(z, lb):
    e = jnp.exp(-jnp.abs(z))
    inv = 1.0 / (1.0 + e)
    pos = z >= 0.0
    sig = jnp.where(pos, inv, e * inv)
    nsig = jnp.where(pos, e * inv, inv)
    f = jnp.maximum(lb, LB_FLOOR) + (1.0 - lb) * sig
    return (1.0 - lb) * nsig, jnp.log2(f)


def _split3(x):
    hi = x.astype(BF16)
    r = x - hi.astype(F32)
    mid = r.astype(BF16)
    lo = (r - mid.astype(F32)).astype(BF16)
    return hi, mid, lo


def _tile_cumsum(g, tri_ref):
    t = tri_ref.shape[0]
    parts = _split3(g)
    tiles = []
    for i in range(g.shape[0] // t):
        tiles.append(sum(_dot(tri_ref[...], p[i * t:(i + 1) * t]) for p in parts))
    return jnp.concatenate(tiles, axis=0)


def _hgrn_in_kernel(x_ref, mod_ref, gpre_ref, w_ref, lbf_ref, lbb_ref, trif_ref, trib_ref,
                    q_ref, v_ref, kf_ref, bf_ref, kb_ref, bb_ref, sg_ref, *, layer_j, kd, hd):
    hb = _pre(x_ref[...], mod_ref, gpre_ref).astype(BF16)
    col = lambda n: _dot(hb, w_ref[:, n * kd:(n + 1) * kd])
    q = _silu(col(0))
    v = col(1)
    kf, lf = _forget_gate(col(2), _lower_bound(lbf_ref, layer_j))
    kb, lbw = _forget_gate(col(3), _lower_bound(lbb_ref, layer_j))
    sg_ref[...] = _silu(col(4)).astype(BF16)
    bf = _tile_cumsum(lf, trif_ref)
    bb = _tile_cumsum(lbw, trib_ref)
    for h in range(HGRN_HEADS):
        sl = slice(h * hd, (h + 1) * hd)
        q_ref[h] = q[:, sl].astype(BF16)
        v_ref[h] = v[:, sl].astype(BF16)
        kf_ref[h] = kf[:, sl].astype(BF16)
        kb_ref[h] = kb[:, sl].astype(BF16)
        bf_ref[h] = bf[:, sl]
        bb_ref[h] = bb[:, sl]


def _hgrn_inputs(x, mod, gpre, w_in, lb_fwd, lb_bwd, layer_j):
    b, l, d = x.shape
    kd = d
    hd = kd // HGRN_HEADS
    tm = _token_tile(l, 512)
    t = _token_tile(tm, SCAN_TILE)
    trif, _ = _scan_consts(t, False)
    trib, _ = _scan_consts(t, True)
    x_spec = pl.BlockSpec((None, tm, d), lambda i, j: (i, j, 0))
    head_spec = pl.BlockSpec((None, HGRN_HEADS, tm, hd), lambda i, j: (i, 0, j, 0))
    heads = lambda dt: jax.ShapeDtypeStruct((b, HGRN_HEADS, l, hd), dt)
    return pl.pallas_call(
        functools.partial(_hgrn_in_kernel, layer_j=layer_j, kd=kd, hd=hd),
        grid=(b, l // tm),
        in_specs=[
            x_spec,
            pl.BlockSpec((None, 3, d), lambda i, j: (i, 0, 0)),
            _const_spec((1, d)),
            _const_spec(w_in.shape),
            _const_spec(lb_fwd.shape),
            _const_spec(lb_bwd.shape),
            _const_spec((t, t)),
            _const_spec((t, t)),
        ],
        out_specs=[head_spec] * 6 + [x_spec],
        out_shape=[heads(BF16), heads(BF16), heads(BF16), heads(F32), heads(BF16), heads(F32),
                   jax.ShapeDtypeStruct((b, l, d), BF16)],
        compiler_params=_params("parallel", "parallel"),
        name="hgrn_in",
    )(x, mod, gpre, w_in, lb_fwd, lb_bwd, trif, trib)


def _scan_consts(t, reverse):
    r = np.arange(t)[:, None]
    c = np.arange(t)[None, :]
    valid = (c >= r) if reverse else (c <= r)
    x = r ^ c
    lvl = np.where(x == 0, 0, np.floor(np.log2(np.maximum(x, 1))).astype(np.int64) + 1)
    return jnp.asarray(valid, BF16), jnp.asarray(np.where(valid, lvl, -1), jnp.int32)


def _level_ref(b, m, reverse):
    t, n = b.shape
    first = m if reverse else m - 1
    if m >= SUBLANES:
        return jnp.concatenate(
            [jnp.broadcast_to(b[p + first:p + first + 1, :], (2 * m, n)) for p in range(0, t, 2 * m)], axis=0)
    b3 = b.reshape(t // SUBLANES, SUBLANES, n)
    rid = lax.broadcasted_iota(jnp.int32, b3.shape, 1)
    ref = None
    for p in range(0, SUBLANES, 2 * m):
        row = jnp.broadcast_to(b3[:, p + first:p + first + 1, :], b3.shape)
        ref = row if ref is None else jnp.where(rid >= p, row, ref)
    return ref.reshape(t, n)


def _neg_abs(d):
    bits = lax.bitcast_convert_type(d, jnp.int32) | jnp.int32(-2 ** 31)
    return lax.bitcast_convert_type(bits, F32)


def _scan_tile(q, k, v, b, st, lvl, reverse):
    t = q.shape[0]
    q32 = q.astype(F32)
    k32 = k.astype(F32)
    scaled = lambda z, e: (z * e).astype(BF16)
    end = 0 if reverse else t - 1
    b_end = b[end:end + 1, :]

    o = _dot_nt(scaled(q32, jnp.exp2(b)), st.astype(BF16))
    new_st = st * jnp.exp2(b_end) + _dot_tn(v, scaled(k32, jnp.exp2(b_end - b)))

    a = jnp.where(lvl == 0, _dot_nt(q, k), 0.0)
    before = pltpu.roll(b, (t - 1) if reverse else 1, 0)
    a = jnp.where(lvl == 1, _dot_nt(scaled(q32, jnp.exp2(_neg_abs(b - before))), k), a)
    m, level = 2, 2
    while m < t:
        e = jnp.exp2(_neg_abs(b - _level_ref(b, m, reverse)))
        a = jnp.where(lvl == level, _dot_nt(scaled(q32, e), scaled(k32, e)), a)
        m, level = 2 * m, level + 1
    return o + _dot(a.astype(BF16), v), new_st


def _scan_kernel(q_ref, k_ref, v_ref, b_ref, s0_ref, lvl_ref, o_ref, sfin_ref, state_ref, *, reverse, n_tiles):
    step = pl.program_id(2)

    @pl.when(step == 0)
    def _():
        state_ref[...] = s0_ref[...]

    lvl = lvl_ref[...]
    for h in range(q_ref.shape[0]):
        o, st = _scan_tile(q_ref[h], k_ref[h], v_ref[h], b_ref[h], state_ref[h], lvl, reverse)
        state_ref[h] = st
        o_ref[h] = o.astype(o_ref.dtype)

    @pl.when(step == n_tiles - 1)
    def _():
        sfin_ref[...] = state_ref[...]


def _scan(q, k, v, bcum, s0, reverse):
    b, h, l, hd = q.shape
    t = _token_tile(l, SCAN_TILE)
    n_tiles = l // t
    hp = SCAN_HEADS
    _, lvl = _scan_consts(t, reverse)
    tile = (lambda j: n_tiles - 1 - j) if reverse else (lambda j: j)
    tok_spec = pl.BlockSpec((None, hp, t, hd), lambda i, hh, j: (i, hh, tile(j), 0))
    st_spec = pl.BlockSpec((None, hp, hd, hd), lambda i, hh, j: (i, hh, 0, 0))
    return pl.pallas_call(
        functools.partial(_scan_kernel, reverse=reverse, n_tiles=n_tiles),
        grid=(b, h // hp, n_tiles),
        in_specs=[tok_spec, tok_spec, tok_spec, tok_spec, st_spec, _const_spec((t, t))],
        out_specs=[tok_spec, st_spec],
        out_shape=[jax.ShapeDtypeStruct((b, h, l, hd), BF16), jax.ShapeDtypeStruct((b, h, hd, hd), F32)],
        scratch_shapes=[pltpu.VMEM((hp, hd, hd), F32)],
        compiler_params=_params("parallel", "parallel", "arbitrary"),
        name="scan_bwd" if reverse else "scan_fwd",
    )(q, k, v, bcum, s0, lvl)


def _hgrn_out_kernel(of_ref, ob_ref, sg_ref, x_ref, mod_ref, gpost_ref, gn_ref, w_ref, o_ref):
    heads = []
    for h in range(HGRN_HEADS):
        heads.append(_rms(of_ref[h].astype(F32) + ob_ref[h].astype(F32), gn_ref[...]))
    o = jnp.concatenate(heads, axis=1)
    y = _dot((o * sg_ref[...].astype(F32)).astype(BF16), w_ref[...])
    o_ref[...] = _post(x_ref[...], y, mod_ref, gpost_ref, 1.0)


def _hgrn_readout(o_f, o_b, sg, x, mod, gpost, g_norm, w_out):
    b, l, d = x.shape
    hd = d // HGRN_HEADS
    tm = _token_tile(l, 512)
    x_spec = pl.BlockSpec((None, tm, d), lambda i, j: (i, j, 0))
    head_spec = pl.BlockSpec((None, HGRN_HEADS, tm, hd), lambda i, j: (i, 0, j, 0))
    return pl.pallas_call(
        _hgrn_out_kernel,
        grid=(b, l // tm),
        in_specs=[
            head_spec, head_spec, x_spec, x_spec,
            pl.BlockSpec((None, 3, d), lambda i, j: (i, 0, 0)),
            _const_spec((1, d)),
            _const_spec((1, hd)),
            _const_spec(w_out.shape),
        ],
        out_specs=x_spec,
        out_shape=jax.ShapeDtypeStruct(x.shape, F32),
        compiler_params=_params("parallel", "parallel"),
        name="hgrn_out",
    )(o_f, o_b, sg, x, mod, gpost, g_norm, w_out)


def kernel(x, c, ctx, c_ctx, ada_w, ada_b, norm_pre, norm_post, ffn_w_in, ffn_w_out, fourier_w_out,
           hgrn_w_in, hgrn_lb_fwd, hgrn_lb_bwd, hgrn_norm, hgrn_w_out):
    bsz, _, d = x.shape
    depth = ada_w.shape[0]
    assert bsz + 1 <= MOD_ROWS

    c_rows = jnp.concatenate([c, c_ctx[None, :], jnp.zeros((MOD_ROWS - bsz - 1, d), F32)], axis=0)
    mod = _adaln(c_rows, ada_w, ada_b).reshape(depth, MOD_ROWS, N_MOD, d)

    ffn_w_in = ffn_w_in.astype(BF16)
    ffn_w_out = ffn_w_out.astype(BF16)
    fourier_w_out = fourier_w_out.astype(BF16)
    hgrn_w_in = hgrn_w_in.astype(BF16)
    hgrn_w_out = hgrn_w_out.astype(BF16)

    for i in range(depth):
        last = i == depth - 1
        is_hgrn = i % 2 == 1
        jm = i // 2
        mx = lambda j: mod[i, :bsz, 3 * j:3 * j + 3]
        mc = lambda j: mod[i, bsz:bsz + 1, 3 * j:3 * j + 3]
        gpre = lambda j: norm_pre[i, j][None, :]
        gpost = lambda j: norm_post[i, j][None, :]
        ctx_mod = lambda j: jnp.broadcast_to(mc(j), (bsz, 3, d))

        x = _ffn(x, mx(0), gpre(0), gpost(0), ffn_w_in[i, 0], ffn_w_out[i, 0])
        if is_hgrn or not last:
            ctx = _ffn(ctx, ctx_mod(0), gpre(0), gpost(0), ffn_w_in[i, 0], ffn_w_out[i, 0])

        if not is_hgrn:
            x = _fourier_latent(x, mx(1), gpre(1), gpost(1), fourier_w_out[jm])
            if not last:
                ctx = _fourier_ctx(ctx, mc(1), gpre(1), gpost(1), fourier_w_out[jm])
        else:
            hd = d // HGRN_HEADS
            zero = jnp.zeros((bsz, HGRN_HEADS, hd, hd), F32)
            qc, vc, kfc, lfc, kbc, lbc, sgc = _hgrn_inputs(
                ctx, ctx_mod(1), gpre(1), hgrn_w_in[jm], hgrn_lb_fwd, hgrn_lb_bwd, jm)
            ocf, s_f = _scan(qc, kfc, vc, lfc, zero, False)
            ocb, s_b = _scan(qc, kbc, vc, lbc, zero, True)
            qx, vx, kfx, lfx, kbx, lbx, sgx = _hgrn_inputs(
                x, mx(1), gpre(1), hgrn_w_in[jm], hgrn_lb_fwd, hgrn_lb_bwd, jm)
            oxf, _ = _scan(qx, kfx, vx, lfx, s_f, False)
            oxb, _ = _scan(qx, kbx, vx, lbx, s_b, True)
            gn = hgrn_norm[jm][None, :]
            x = _hgrn_readout(oxf, oxb, sgx, x, mx(1), gpost(1), gn, hgrn_w_out[jm])
            if not last:
                ctx = _hgrn_readout(ocf, ocb, sgc, ctx, ctx_mod(1), gpost(1), gn, hgrn_w_out[jm])

        x = _ffn(x, mx(2), gpre(2), gpost(2), ffn_w_in[i, 1], ffn_w_out[i, 1])
        if not last:
            ctx = _ffn(ctx, ctx_mod(2), gpre(2), gpost(2), ffn_w_in[i, 1], ffn_w_out[i, 1])
    return x
```

```python
import functools

import jax
import jax.numpy as jnp
import numpy as np
from jax import lax
from jax.experimental import pallas as pl
from jax.experimental.pallas import tpu as pltpu

F32 = jnp.float32
BF16 = jnp.bfloat16

GRID_W = 64
FOURIER_GROUPS = 4
HGRN_HEADS = 8
N_MOD = 9
HALF = 0.5
NORM_EPS = 1e-6
LB_FLOOR = 1e-30

VMEM_LIMIT_V7X = 56 * 1024 * 1024
SUBLANES = 8
LANES = 128
SCAN_TILE = 256
SCAN_HEADS = 4
MOD_ROWS = 16


def _params(*sem):
    return pltpu.CompilerParams(dimension_semantics=sem, vmem_limit_bytes=VMEM_LIMIT_V7X)


def _const_spec(shape):
    n = len(shape)
    return pl.BlockSpec(shape, lambda *_: (0,) * n, pipeline_mode=pl.Buffered(1))


def _rms(x, g):
    ms = jnp.mean(x * x, axis=-1, keepdims=True)
    return x * lax.rsqrt(ms + NORM_EPS) * g


def _pre(x, mod_ref, g_ref):
    return _rms(x, g_ref[...]) * (1.0 + mod_ref[1:2, :]) + mod_ref[0:1, :]


def _post(x, y, mod_ref, g_ref, w):
    return x + w * mod_ref[2:3, :] * _rms(y, g_ref[...])


def _silu(x):
    return x * jax.nn.sigmoid(x)


def _dot(a, b):
    return jnp.dot(a, b, preferred_element_type=F32)


def _dot_nt(a, b):
    return lax.dot_general(a, b, (((1,), (1,)), ((), ())), preferred_element_type=F32)


def _dot_tn(a, b):
    return lax.dot_general(a, b, (((0,), (0,)), ((), ())), preferred_element_type=F32)


def _adaln_kernel(c_ref, w_ref, b_ref, o_ref):
    sc = _silu(c_ref[...]).astype(BF16)
    o_ref[...] = _dot(sc, w_ref[...].astype(BF16)) + b_ref[...]


def _adaln(c_rows, ada_w, ada_b):
    depth, d, n = ada_w.shape
    tn = 1024
    return pl.pallas_call(
        _adaln_kernel,
        grid=(depth, n // tn),
        in_specs=[
            pl.BlockSpec((MOD_ROWS, d), lambda i, j: (0, 0)),
            pl.BlockSpec((None, d, tn), lambda i, j: (i, 0, j)),
            pl.BlockSpec((None, 1, tn), lambda i, j: (i, 0, j)),
        ],
        out_specs=pl.BlockSpec((None, MOD_ROWS, tn), lambda i, j: (i, 0, j)),
        out_shape=jax.ShapeDtypeStruct((depth, MOD_ROWS, n), F32),
        compiler_params=_params("parallel", "parallel"),
        name="adaln",
    )(c_rows, ada_w, ada_b.reshape(depth, 1, n))


def _ffn_kernel(x_ref, mod_ref, gpre_ref, gpost_ref, win_ref, wout_ref, o_ref, *, d_ff, n_chunks):
    x = x_ref[...]
    hb = _pre(x, mod_ref, gpre_ref).astype(BF16)
    fc = d_ff // n_chunks
    acc = None
    for ci in range(n_chunks):
        gate = _dot(hb, win_ref[:, ci * fc:(ci + 1) * fc])
        up = _dot(hb, win_ref[:, d_ff + ci * fc:d_ff + (ci + 1) * fc])
        a = (_silu(gate) * up).astype(BF16)
        y = _dot(a, wout_ref[ci * fc:(ci + 1) * fc, :])
        acc = y if acc is None else acc + y
    o_ref[...] = _post(x, acc, mod_ref, gpost_ref, HALF)


def _token_tile(n_tokens, want):
    return want if n_tokens % want == 0 else n_tokens


def _ffn(s, mod, gpre, gpost, w_in, w_out):
    b, l, d = s.shape
    d_ff = w_out.shape[0]
    tm = _token_tile(l, 512)
    return pl.pallas_call(
        functools.partial(_ffn_kernel, d_ff=d_ff, n_chunks=2),
        grid=(b, l // tm),
        in_specs=[
            pl.BlockSpec((None, tm, d), lambda i, j: (i, j, 0)),
            pl.BlockSpec((None, 3, d), lambda i, j: (i, 0, 0)),
            _const_spec((1, d)),
            _const_spec((1, d)),
            _const_spec(w_in.shape),
            _const_spec(w_out.shape),
        ],
        out_specs=pl.BlockSpec((None, tm, d), lambda i, j: (i, j, 0)),
        out_shape=jax.ShapeDtypeStruct(s.shape, F32),
        compiler_params=_params("parallel", "parallel"),
        name="ffn",
    )(s, mod, gpre, gpost, w_in, w_out)


def _dft_cos_sin(n):
    k = np.arange(n)
    ang = 2.0 * np.pi * ((k[:, None] * k[None, :]) % n) / n
    s = 1.0 / np.sqrt(n)
    return np.cos(ang) * s, np.sin(ang) * s


def _fourier_consts(gd):
    c, s = _dft_cos_sin(gd)
    chan = np.concatenate([c, s], axis=1)
    c, s = _dft_cos_sin(GRID_W)
    col = np.block([[c, -s], [s, c]])
    return jnp.asarray(chan, BF16), jnp.asarray(col, BF16)


def _channel_dft(hb, chan_ref, gd):
    us, vs = [], []
    for g in range(FOURIER_GROUPS):
        uv = _dot(hb[:, g * gd:(g + 1) * gd], chan_ref[...])
        us.append(uv[:, :gd])
        vs.append(uv[:, gd:])
    return (jnp.concatenate(us, axis=1).astype(BF16), jnp.concatenate(vs, axis=1).astype(BF16))


def _pack_pair(a, b):
    hi = lax.bitcast_convert_type(a.astype(BF16).astype(F32), jnp.uint32)
    lo = lax.bitcast_convert_type(b.astype(BF16).astype(F32), jnp.uint32)
    return hi | (lo >> 16)


def _unpack_pair(p):
    a = lax.bitcast_convert_type(p & jnp.uint32(0xFFFF0000), F32)
    b = lax.bitcast_convert_type(p << 16, F32)
    return a.astype(BF16), b.astype(BF16)


def _fourier_cols_kernel(x_ref, mod_ref, gpre_ref, chan_ref, col_ref, ab_ref, *, rows_per_step, gd):
    hb = _pre(x_ref[...], mod_ref, gpre_ref).astype(BF16)
    u, v = _channel_dft(hb, chan_ref, gd)
    for r in range(rows_per_step):
        sl = slice(r * GRID_W, (r + 1) * GRID_W)
        ab = _dot(col_ref[...], jnp.concatenate([u[sl], v[sl]], axis=0))
        ab_ref[sl, :] = _pack_pair(ab[:GRID_W], ab[GRID_W:])


def _fourier_rows_kernel(ab_ref, x_ref, mod_ref, gpost_ref, rowk_ref, w_ref, o_ref):
    rows, cps, d = x_ref.shape
    flat = lambda v: v.reshape(rows * cps, d)
    a, b = _unpack_pair(flat(ab_ref[...]))
    y = _dot(rowk_ref[...], jnp.concatenate([a, b], axis=0)).astype(BF16)
    res = _post(flat(x_ref[...]), _dot(y, w_ref[...]), mod_ref, gpost_ref, 1.0)
    o_ref[...] = res.reshape(rows, cps, d)


def _fourier_latent(x, mod, gpre, gpost, w_out):
    b, l, d = x.shape
    rows = l // GRID_W
    gd = d // FOURIER_GROUPS
    chan, col = _fourier_consts(gd)
    rps = 4 if rows % 4 == 0 else 1
    tm = rps * GRID_W
    ab = pl.pallas_call(
        functools.partial(_fourier_cols_kernel, rows_per_step=rps, gd=gd),
        grid=(b, l // tm),
        in_specs=[
            pl.BlockSpec((None, tm, d), lambda i, j: (i, j, 0)),
            pl.BlockSpec((None, 3, d), lambda i, j: (i, 0, 0)),
            _const_spec((1, d)),
            _const_spec(chan.shape),
            _const_spec(col.shape),
        ],
        out_specs=pl.BlockSpec((None, tm, d), lambda i, j: (i, j, 0)),
        out_shape=jax.ShapeDtypeStruct((b, l, d), jnp.uint32),
        compiler_params=_params("parallel", "parallel"),
        name="fourier_cols",
    )(x, mod, gpre, chan, col)

    c, s = _dft_cos_sin(rows)
    cps = SUBLANES
    eye = np.eye(cps)
    rowk = jnp.asarray(np.concatenate([np.kron(c, eye), -np.kron(s, eye)], axis=1), BF16)
    grid_view = lambda t: t.reshape(b, rows, GRID_W, d)
    blk = pl.BlockSpec((None, rows, cps, d), lambda i, j: (i, 0, j, 0))
    out = pl.pallas_call(
        _fourier_rows_kernel,
        grid=(b, GRID_W // cps),
        in_specs=[
            blk, blk,
            pl.BlockSpec((None, 3, d), lambda i, j: (i, 0, 0)),
            _const_spec((1, d)),
            _const_spec(rowk.shape),
            _const_spec(w_out.shape),
        ],
        out_specs=blk,
        out_shape=jax.ShapeDtypeStruct((b, rows, GRID_W, d), F32),
        compiler_params=_params("parallel", "parallel"),
        name="fourier_rows",
    )(grid_view(ab), grid_view(x), mod, gpost, rowk, w_out)
    return out.reshape(b, l, d)


def _fourier_ctx_kernel(x_ref, mod_ref, gpre_ref, gpost_ref, chan_ref, seq_ref, w_ref, o_ref, *, gd):
    x = x_ref[...]
    hb = _pre(x, mod_ref, gpre_ref).astype(BF16)
    u, v = _channel_dft(hb, chan_ref, gd)
    y = _dot(seq_ref[...], jnp.concatenate([u, v], axis=0)).astype(BF16)
    o_ref[...] = _post(x, _dot(y, w_ref[...]), mod_ref, gpost_ref, 1.0)


def _fourier_ctx(x, mod, gpre, gpost, w_out):
    b, l, d = x.shape
    gd = d // FOURIER_GROUPS
    chan, _ = _fourier_consts(gd)
    c, s = _dft_cos_sin(l)
    seqm = jnp.asarray(np.concatenate([c, -s], axis=1), BF16)
    return pl.pallas_call(
        functools.partial(_fourier_ctx_kernel, gd=gd),
        grid=(b,),
        in_specs=[
            pl.BlockSpec((None, l, d), lambda i: (i, 0, 0)),
            pl.BlockSpec((None, 3, d), lambda i: (0, 0, 0)),
            _const_spec((1, d)),
            _const_spec((1, d)),
            _const_spec(chan.shape),
            _const_spec(seqm.shape),
            _const_spec(w_out.shape),
        ],
        out_specs=pl.BlockSpec((None, l, d), lambda i: (i, 0, 0)),
        out_shape=jax.ShapeDtypeStruct(x.shape, F32),
        compiler_params=_params("parallel"),
        name="fourier_ctx",
    )(x, mod, gpre, gpost, chan, seqm, w_out)


def _lower_bound(logits_ref, j):
    lg = logits_ref[...]
    e = jnp.exp(lg - jnp.max(lg, axis=0, keepdims=True))
    p = e / jnp.sum(e, axis=0, keepdims=True)
    lb = jnp.zeros_like(p[0:1])
    for i in range(1, j + 1):
        lb = lb + p[i:i + 1]
    return lb


def _forget_gate(z, lb):
    e = jnp.exp(-jnp.abs(z))
    inv = 1.0 / (1.0 + e)
    pos = z >= 0.0
    sig = jnp.where(pos, inv, e * inv)
    nsig = jnp.where(pos, e * inv, inv)
    f = jnp.maximum(lb, LB_FLOOR) + (1.0 - lb) * sig
    return (1.0 - lb) * nsig, jnp.log2(f)


def _split2(x):
    hi = x.astype(BF16)
    lo = (x - hi.astype(F32)).astype(BF16)
    return hi, lo


def _tile_cumsum(g, tri_ref):
    t = tri_ref.shape[0]
    parts = _split2(g)
    tiles = []
    for i in range(g.shape[0] // t):
        tiles.append(sum(_dot(tri_ref[...], p[i * t:(i + 1) * t]) for p in parts))
    return jnp.concatenate(tiles, axis=0)


def _hgrn_in_kernel(x_ref, mod_ref, gpre_ref, w_ref, lbf_ref, lbb_ref, trif_ref, trib_ref,
                    q_ref, v_ref, kf_ref, bf_ref, kb_ref, bb_ref, sg_ref, *, layer_j, kd, hd):
    hb = _pre(x_ref[...], mod_ref, gpre_ref).astype(BF16)
    col = lambda n: _dot(hb, w_ref[:, n * kd:(n + 1) * kd])
    q = _silu(col(0))
    v = col(1)
    kf, lf = _forget_gate(col(2), _lower_bound(lbf_ref, layer_j))
    kb, lbw = _forget_gate(col(3), _lower_bound(lbb_ref, layer_j))
    sg_ref[...] = _silu(col(4)).astype(BF16)
    bf = _tile_cumsum(lf, trif_ref)
    bb = _tile_cumsum(lbw, trib_ref)
    for h in range(HGRN_HEADS):
        sl = slice(h * hd, (h + 1) * hd)
        q_ref[h] = q[:, sl].astype(BF16)
        v_ref[h] = v[:, sl].astype(BF16)
        kf_ref[h] = kf[:, sl].astype(BF16)
        kb_ref[h] = kb[:, sl].astype(BF16)
        bf_ref[h] = bf[:, sl]
        bb_ref[h] = bb[:, sl]


def _hgrn_inputs(x, mod, gpre, w_in, lb_fwd, lb_bwd, layer_j):
    b, l, d = x.shape
    kd = d
    hd = kd // HGRN_HEADS
    tm = _token_tile(l, 512)
    t = _token_tile(tm, SCAN_TILE)
    trif, _ = _scan_consts(t, False)
    trib, _ = _scan_consts(t, True)
    x_spec = pl.BlockSpec((None, tm, d), lambda i, j: (i, j, 0))
    head_spec = pl.BlockSpec((None, HGRN_HEADS, tm, hd), lambda i, j: (i, 0, j, 0))
    heads = lambda dt: jax.ShapeDtypeStruct((b, HGRN_HEADS, l, hd), dt)
    return pl.pallas_call(
        functools.partial(_hgrn_in_kernel, layer_j=layer_j, kd=kd, hd=hd),
        grid=(b, l // tm),
        in_specs=[
            x_spec,
            pl.BlockSpec((None, 3, d), lambda i, j: (i, 0, 0)),
            _const_spec((1, d)),
            _const_spec(w_in.shape),
            _const_spec(lb_fwd.shape),
            _const_spec(lb_bwd.shape),
            _const_spec((t, t)),
            _const_spec((t, t)),
        ],
        out_specs=[head_spec] * 6 + [x_spec],
        out_shape=[heads(BF16), heads(BF16), heads(BF16), heads(F32), heads(BF16), heads(F32),
                   jax.ShapeDtypeStruct((b, l, d), BF16)],
        compiler_params=_params("parallel", "parallel"),
        name="hgrn_in",
    )(x, mod, gpre, w_in, lb_fwd, lb_bwd, trif, trib)


def _scan_consts(n, reverse):
    r = np.arange(n)[:, None]
    c = np.arange(n)[None, :]
    valid = (c >= r) if reverse else (c <= r)
    x = r ^ c
    lvl = np.where(x == 0, 0, np.floor(np.log2(np.maximum(x, 1))).astype(np.int64) + 1)
    fine = np.where(valid & (x < SUBLANES), lvl, -1)
    return jnp.asarray(valid, BF16), jnp.asarray(fine, jnp.int32)


def _level_ref(b, m, reverse):
    t, n = b.shape
    first = m if reverse else m - 1
    b3 = b.reshape(t // SUBLANES, SUBLANES, n)
    rid = lax.broadcasted_iota(jnp.int32, b3.shape, 1)
    ref = None
    for p in range(0, SUBLANES, 2 * m):
        row = jnp.broadcast_to(b3[:, p + first:p + first + 1, :], b3.shape)
        ref = row if ref is None else jnp.where(rid >= p, row, ref)
    return ref.reshape(t, n)


def _neg_abs(d):
    bits = lax.bitcast_convert_type(d, jnp.int32) | jnp.int32(-2 ** 31)
    return lax.bitcast_convert_type(bits, F32)


def _scaled(z, e):
    return (z * e).astype(BF16)


def _halves(lo, m, reverse):
    return (lo, lo + m, lo + m) if reverse else (lo + m, lo, lo + m - 1)


def _strip_scores(q, k, q32, k32, b, fine, lane, reverse):
    n = q.shape[0]
    a = jnp.where(fine == 0, _dot_nt(q, k), 0.0)
    before = pltpu.roll(b, (n - 1) if reverse else 1, 0)
    a = jnp.where(fine == 1, _dot_nt(_scaled(q32, jnp.exp2(_neg_abs(b - before))), k), a)
    m, level = 2, 2
    while m < SUBLANES:
        e = jnp.exp2(_neg_abs(b - _level_ref(b, m, reverse)))
        a = jnp.where(fine == level, _dot_nt(_scaled(q32, e), _scaled(k32, e)), a)
        m, level = 2 * m, level + 1
    blocks = [a[i:i + SUBLANES] for i in range(0, n, SUBLANES)]
    while m < n:
        ql, kf = [], []
        for lo in range(0, n, 2 * m):
            late, early, first = _halves(lo, m, reverse)
            ref = b[first:first + 1, :]
            ql.append(_scaled(q32[late:late + m], jnp.exp2(b[late:late + m] - ref)))
            ke = _scaled(k32[early:early + m], jnp.exp2(ref - b[early:early + m]))
            kf += [k[late:late + m], ke] if reverse else [ke, k[late:late + m]]
        s_m = _dot_nt(jnp.concatenate(ql, axis=0), jnp.concatenate(kf, axis=0))
        for pi, lo in enumerate(range(0, n, 2 * m)):
            late, early, _ = _halves(lo, m, reverse)
            inside = (lane >= early) & (lane < early + m)
            for i in range(0, m, SUBLANES):
                rb = (late + i) // SUBLANES
                blocks[rb] = jnp.where(inside, s_m[pi * m + i:pi * m + i + SUBLANES], blocks[rb])
        m *= 2
    return jnp.concatenate(blocks, axis=0)


def _scan_tile(q, k, v, b, st, fine, reverse):
    t = q.shape[0]
    n = fine.shape[0]
    q32 = q.astype(F32)
    k32 = k.astype(F32)
    end = 0 if reverse else t - 1
    b_end = b[end:end + 1, :]

    o = _dot_nt(_scaled(q32, jnp.exp2(b)), st.astype(BF16))
    new_st = st * jnp.exp2(b_end) + _dot_tn(v, _scaled(k32, jnp.exp2(b_end - b)))

    lane = lax.broadcasted_iota(jnp.int32, (SUBLANES, n), 1)
    strips = range(0, t, n)
    score = {(i, i): _strip_scores(q[i:i + n], k[i:i + n], q32[i:i + n], k32[i:i + n], b[i:i + n],
                                   fine, lane, reverse) for i in strips}
    m = n
    while m < t:
        for lo in range(0, t, 2 * m):
            late, early, first = _halves(lo, m, reverse)
            ref = b[first:first + 1, :]
            ql = _scaled(q32[late:late + m], jnp.exp2(b[late:late + m] - ref))
            ke = _scaled(k32[early:early + m], jnp.exp2(ref - b[early:early + m]))
            s_m = _dot_nt(ql, ke)
            for i in range(0, m, n):
                for j in range(0, m, n):
                    score[(late + i, early + j)] = s_m[i:i + n, j:j + n]
        m *= 2
    outs = []
    for i in strips:
        keys = [j for j in strips if (i, j) in score]
        a = jnp.concatenate([score[(i, j)] for j in keys], axis=1).astype(BF16)
        outs.append(_dot(a, jnp.concatenate([v[j:j + n] for j in keys], axis=0)))
    return o + jnp.concatenate(outs, axis=0), new_st


def _scan_kernel(q_ref, k_ref, v_ref, b_ref, s0_ref, fine_ref, o_ref, sfin_ref, state_ref, *, reverse, n_tiles):
    step = pl.program_id(2)

    @pl.when(step == 0)
    def _():
        state_ref[...] = s0_ref[...]

    fine = fine_ref[...]
    for h in range(q_ref.shape[0]):
        o, st = _scan_tile(q_ref[h], k_ref[h], v_ref[h], b_ref[h], state_ref[h], fine, reverse)
        state_ref[h] = st
        o_ref[h] = o.astype(o_ref.dtype)

    @pl.when(step == n_tiles - 1)
    def _():
        sfin_ref[...] = state_ref[...]


def _scan(q, k, v, bcum, s0, reverse):
    b, h, l, hd = q.shape
    t = _token_tile(l, SCAN_TILE)
    n_tiles = l // t
    hp = SCAN_HEADS
    _, fine = _scan_consts(LANES, reverse)
    tile = (lambda j: n_tiles - 1 - j) if reverse else (lambda j: j)
    tok_spec = pl.BlockSpec((None, hp, t, hd), lambda i, hh, j: (i, hh, tile(j), 0))
    st_spec = pl.BlockSpec((None, hp, hd, hd), lambda i, hh, j: (i, hh, 0, 0))
    return pl.pallas_call(
        functools.partial(_scan_kernel, reverse=reverse, n_tiles=n_tiles),
        grid=(b, h // hp, n_tiles),
        in_specs=[tok_spec, tok_spec, tok_spec, tok_spec, st_spec, _const_spec(fine.shape)],
        out_specs=[tok_spec, st_spec],
        out_shape=[jax.ShapeDtypeStruct((b, h, l, hd), BF16), jax.ShapeDtypeStruct((b, h, hd, hd), F32)],
        scratch_shapes=[pltpu.VMEM((hp, hd, hd), F32)],
        compiler_params=_params("parallel", "parallel", "arbitrary"),
        name="scan_bwd" if reverse else "scan_fwd",
    )(q, k, v, bcum, s0, fine)


def _hgrn_out_kernel(of_ref, ob_ref, sg_ref, x_ref, mod_ref, gpost_ref, gn_ref, w_ref, o_ref):
    heads = []
    for h in range(HGRN_HEADS):
        heads.append(_rms(of_ref[h].astype(F32) + ob_ref[h].astype(F32), gn_ref[...]))
    o = jnp.concatenate(heads, axis=1)
    y = _dot((o * sg_ref[...].astype(F32)).astype(BF16), w_ref[...])
    o_ref[...] = _post(x_ref[...], y, mod_ref, gpost_ref, 1.0)


def _hgrn_readout(o_f, o_b, sg, x, mod, gpost, g_norm, w_out):
    b, l, d = x.shape
    hd = d // HGRN_HEADS
    tm = _token_tile(l, 512)
    x_spec = pl.BlockSpec((None, tm, d), lambda i, j: (i, j, 0))
    head_spec = pl.BlockSpec((None, HGRN_HEADS, tm, hd), lambda i, j: (i, 0, j, 0))
    return pl.pallas_call(
        _hgrn_out_kernel,
        grid=(b, l // tm),
        in_specs=[
            head_spec, head_spec, x_spec, x_spec,
            pl.BlockSpec((None, 3, d), lambda i, j: (i, 0, 0)),
            _const_spec((1, d)),
            _const_spec((1, hd)),
            _const_spec(w_out.shape),
        ],
        out_specs=x_spec,
        out_shape=jax.ShapeDtypeStruct(x.shape, F32),
        compiler_params=_params("parallel", "parallel"),
        name="hgrn_out",
    )(o_f, o_b, sg, x, mod, gpost, g_norm, w_out)


def kernel(x, c, ctx, c_ctx, ada_w, ada_b, norm_pre, norm_post, ffn_w_in, ffn_w_out, fourier_w_out,
           hgrn_w_in, hgrn_lb_fwd, hgrn_lb_bwd, hgrn_norm, hgrn_w_out):
    bsz, _, d = x.shape
    depth = ada_w.shape[0]
    assert bsz + 1 <= MOD_ROWS

    c_rows = jnp.concatenate([c, c_ctx[None, :], jnp.zeros((MOD_ROWS - bsz - 1, d), F32)], axis=0)
    mod = _adaln(c_rows, ada_w, ada_b).reshape(depth, MOD_ROWS, N_MOD, d)

    ffn_w_in = ffn_w_in.astype(BF16)
    ffn_w_out = ffn_w_out.astype(BF16)
    fourier_w_out = fourier_w_out.astype(BF16)
    hgrn_w_in = hgrn_w_in.astype(BF16)
    hgrn_w_out = hgrn_w_out.astype(BF16)

    for i in range(depth):
        last = i == depth - 1
        is_hgrn = i % 2 == 1
        jm = i // 2
        mx = lambda j: mod[i, :bsz, 3 * j:3 * j + 3]
        mc = lambda j: mod[i, bsz:bsz + 1, 3 * j:3 * j + 3]
        gpre = lambda j: norm_pre[i, j][None, :]
        gpost = lambda j: norm_post[i, j][None, :]
        ctx_mod = lambda j: jnp.broadcast_to(mc(j), (bsz, 3, d))

        x = _ffn(x, mx(0), gpre(0), gpost(0), ffn_w_in[i, 0], ffn_w_out[i, 0])
        if is_hgrn or not last:
            ctx = _ffn(ctx, ctx_mod(0), gpre(0), gpost(0), ffn_w_in[i, 0], ffn_w_out[i, 0])

        if not is_hgrn:
            x = _fourier_latent(x, mx(1), gpre(1), gpost(1), fourier_w_out[jm])
            if not last:
                ctx = _fourier_ctx(ctx, mc(1), gpre(1), gpost(1), fourier_w_out[jm])
        else:
            hd = d // HGRN_HEADS
            zero = jnp.zeros((bsz, HGRN_HEADS, hd, hd), F32)
            qc, vc, kfc, bfc, kbc, bbc, sgc = _hgrn_inputs(
                ctx, ctx_mod(1), gpre(1), hgrn_w_in[jm], hgrn_lb_fwd, hgrn_lb_bwd, jm)
            ocf, s_f = _scan(qc, kfc, vc, bfc, zero, False)
            ocb, s_b = _scan(qc, kbc, vc, bbc, zero, True)
            qx, vx, kfx, bfx, kbx, bbx, sgx = _hgrn_inputs(
                x, mx(1), gpre(1), hgrn_w_in[jm], hgrn_lb_fwd, hgrn_lb_bwd, jm)
            oxf, _ = _scan(qx, kfx, vx, bfx, s_f, False)
            oxb, _ = _scan(qx, kbx, vx, bbx, s_b, True)
            gn = hgrn_norm[jm][None, :]
            x = _hgrn_readout(oxf, oxb, sgx, x, mx(1), gpost(1), gn, hgrn_w_out[jm])
            if not last:
                ctx = _hgrn_readout(ocf, ocb, sgc, ctx, ctx_mod(1), gpost(1), gn, hgrn_w_out[jm])

        x = _ffn(x, mx(2), gpre(2), gpost(2), ffn_w_in[i, 1], ffn_w_out[i, 1])
        if not last:
            ctx = _ffn(ctx, ctx_mod(2), gpre(2), gpost(2), ffn_w_in[i, 1], ffn_w_out[i, 1])
    return x
```

```python
import functools

import jax
import jax.numpy as jnp
import numpy as np
from jax import lax
from jax.experimental import pallas as pl
from jax.experimental.pallas import tpu as pltpu

F32 = jnp.float32
BF16 = jnp.bfloat16

GRID_W = 64
FOURIER_GROUPS = 4
HGRN_HEADS = 8
N_MOD = 9
HALF = 0.5
NORM_EPS = 1e-6
LB_FLOOR = 1e-30

VMEM_LIMIT_V7X = 56 * 1024 * 1024
SUBLANES = 8
LANES = 128
MXU_COLS = 256
SCAN_TILE = 256
SCAN_HEADS = 8
MOD_ROWS = 16


def _params(*sem):
    return pltpu.CompilerParams(dimension_semantics=sem, vmem_limit_bytes=VMEM_LIMIT_V7X)


def _const_spec(shape):
    n = len(shape)
    return pl.BlockSpec(shape, lambda *_: (0,) * n, pipeline_mode=pl.Buffered(1))


def _rms(x, g):
    ms = jnp.mean(x * x, axis=-1, keepdims=True)
    return x * lax.rsqrt(ms + NORM_EPS) * g


def _pre(x, mod_ref, g_ref):
    return _rms(x, g_ref[...]) * (1.0 + mod_ref[1:2, :]) + mod_ref[0:1, :]


def _post(x, y, mod_ref, g_ref, w):
    return x + w * mod_ref[2:3, :] * _rms(y, g_ref[...])


def _silu(x):
    return x * jax.nn.sigmoid(x)


def _dot(a, b):
    return jnp.dot(a, b, preferred_element_type=F32)


def _dot_nt(a, b):
    return lax.dot_general(a, b, (((1,), (1,)), ((), ())), preferred_element_type=F32)


def _dot_tn(a, b):
    return lax.dot_general(a, b, (((0,), (0,)), ((), ())), preferred_element_type=F32)


def _adaln_kernel(c_ref, w_ref, b_ref, o_ref):
    sc = _silu(c_ref[...]).astype(BF16)
    o_ref[...] = _dot(sc, w_ref[...].astype(BF16)) + b_ref[...]


def _adaln(c_rows, ada_w, ada_b):
    depth, d, n = ada_w.shape
    tn = 1024
    return pl.pallas_call(
        _adaln_kernel,
        grid=(depth, n // tn),
        in_specs=[
            pl.BlockSpec((MOD_ROWS, d), lambda i, j: (0, 0)),
            pl.BlockSpec((None, d, tn), lambda i, j: (i, 0, j)),
            pl.BlockSpec((None, 1, tn), lambda i, j: (i, 0, j)),
        ],
        out_specs=pl.BlockSpec((None, MOD_ROWS, tn), lambda i, j: (i, 0, j)),
        out_shape=jax.ShapeDtypeStruct((depth, MOD_ROWS, n), F32),
        compiler_params=_params("parallel", "parallel"),
        name="adaln",
    )(c_rows, ada_w, ada_b.reshape(depth, 1, n))


def _ffn_kernel(x_ref, mod_ref, gpre_ref, gpost_ref, win_ref, wout_ref, o_ref, *, d_ff):
    x = x_ref[...]
    hb = _pre(x, mod_ref, gpre_ref).astype(BF16)
    fc = MXU_COLS
    acc = None
    for ci in range(d_ff // fc):
        gate = _dot(hb, win_ref[:, ci * fc:(ci + 1) * fc])
        up = _dot(hb, win_ref[:, d_ff + ci * fc:d_ff + (ci + 1) * fc])
        a = (_silu(gate) * up).astype(BF16)
        y = _dot(a, wout_ref[ci * fc:(ci + 1) * fc, :])
        acc = y if acc is None else acc + y
    o_ref[...] = _post(x, acc, mod_ref, gpost_ref, HALF)


def _token_tile(n_tokens, want):
    return want if n_tokens % want == 0 else n_tokens


def _ffn(s, mod, gpre, gpost, w_in, w_out):
    b, l, d = s.shape
    d_ff = w_out.shape[0]
    assert d_ff % MXU_COLS == 0
    tm = _token_tile(l, 1024)
    return pl.pallas_call(
        functools.partial(_ffn_kernel, d_ff=d_ff),
        grid=(b, l // tm),
        in_specs=[
            pl.BlockSpec((None, tm, d), lambda i, j: (i, j, 0)),
            pl.BlockSpec((None, 3, d), lambda i, j: (i, 0, 0)),
            _const_spec((1, d)),
            _const_spec((1, d)),
            _const_spec(w_in.shape),
            _const_spec(w_out.shape),
        ],
        out_specs=pl.BlockSpec((None, tm, d), lambda i, j: (i, j, 0)),
        out_shape=jax.ShapeDtypeStruct(s.shape, F32),
        compiler_params=_params("parallel", "parallel"),
        name="ffn",
    )(s, mod, gpre, gpost, w_in, w_out)


def _dft_cos_sin(n):
    k = np.arange(n)
    ang = 2.0 * np.pi * ((k[:, None] * k[None, :]) % n) / n
    s = 1.0 / np.sqrt(n)
    return np.cos(ang) * s, np.sin(ang) * s


def _fourier_consts(gd):
    c, s = _dft_cos_sin(gd)
    chan = np.concatenate([c, s], axis=1)
    c, s = _dft_cos_sin(GRID_W)
    col = np.block([[c, -s], [s, c]])
    return jnp.asarray(chan, BF16), jnp.asarray(col, BF16)


def _channel_dft(hb, chan_ref, gd):
    us, vs = [], []
    for g in range(FOURIER_GROUPS):
        uv = _dot(hb[:, g * gd:(g + 1) * gd], chan_ref[...])
        us.append(uv[:, :gd])
        vs.append(uv[:, gd:])
    return (jnp.concatenate(us, axis=1).astype(BF16), jnp.concatenate(vs, axis=1).astype(BF16))


def _pack_pair(a, b):
    hi = lax.bitcast_convert_type(a.astype(BF16).astype(F32), jnp.uint32)
    lo = lax.bitcast_convert_type(b.astype(BF16).astype(F32), jnp.uint32)
    return hi | (lo >> 16)


def _unpack_pair(p):
    a = lax.bitcast_convert_type(p & jnp.uint32(0xFFFF0000), F32)
    b = lax.bitcast_convert_type(p << 16, F32)
    return a.astype(BF16), b.astype(BF16)


def _fourier_cols_kernel(x_ref, mod_ref, gpre_ref, chan_ref, col_ref, ab_ref, *, rows_per_step, gd):
    hb = _pre(x_ref[...], mod_ref, gpre_ref).astype(BF16)
    u, v = _channel_dft(hb, chan_ref, gd)
    for r in range(rows_per_step):
        sl = slice(r * GRID_W, (r + 1) * GRID_W)
        ab = _dot(col_ref[...], jnp.concatenate([u[sl], v[sl]], axis=0))
        ab_ref[sl, :] = _pack_pair(ab[:GRID_W], ab[GRID_W:])


def _fourier_rows_kernel(ab_ref, x_ref, mod_ref, gpost_ref, rowk_ref, w_ref, o_ref):
    rows, cps, d = x_ref.shape
    flat = lambda v: v.reshape(rows * cps, d)
    a, b = _unpack_pair(flat(ab_ref[...]))
    y = _dot(rowk_ref[...], jnp.concatenate([a, b], axis=0)).astype(BF16)
    res = _post(flat(x_ref[...]), _dot(y, w_ref[...]), mod_ref, gpost_ref, 1.0)
    o_ref[...] = res.reshape(rows, cps, d)


def _fourier_latent(x, mod, gpre, gpost, w_out):
    b, l, d = x.shape
    rows = l // GRID_W
    gd = d // FOURIER_GROUPS
    chan, col = _fourier_consts(gd)
    rps = 8 if rows % 8 == 0 else 1
    tm = rps * GRID_W
    ab = pl.pallas_call(
        functools.partial(_fourier_cols_kernel, rows_per_step=rps, gd=gd),
        grid=(b, l // tm),
        in_specs=[
            pl.BlockSpec((None, tm, d), lambda i, j: (i, j, 0)),
            pl.BlockSpec((None, 3, d), lambda i, j: (i, 0, 0)),
            _const_spec((1, d)),
            _const_spec(chan.shape),
            _const_spec(col.shape),
        ],
        out_specs=pl.BlockSpec((None, tm, d), lambda i, j: (i, j, 0)),
        out_shape=jax.ShapeDtypeStruct((b, l, d), jnp.uint32),
        compiler_params=_params("parallel", "parallel"),
        name="fourier_cols",
    )(x, mod, gpre, chan, col)

    c, s = _dft_cos_sin(rows)
    cps = SUBLANES
    eye = np.eye(cps)
    rowk = jnp.asarray(np.concatenate([np.kron(c, eye), -np.kron(s, eye)], axis=1), BF16)
    grid_view = lambda t: t.reshape(b, rows, GRID_W, d)
    blk = pl.BlockSpec((None, rows, cps, d), lambda i, j: (i, 0, j, 0))
    out = pl.pallas_call(
        _fourier_rows_kernel,
        grid=(b, GRID_W // cps),
        in_specs=[
            blk, blk,
            pl.BlockSpec((None, 3, d), lambda i, j: (i, 0, 0)),
            _const_spec((1, d)),
            _const_spec(rowk.shape),
            _const_spec(w_out.shape),
        ],
        out_specs=blk,
        out_shape=jax.ShapeDtypeStruct((b, rows, GRID_W, d), F32),
        compiler_params=_params("parallel", "parallel"),
        name="fourier_rows",
    )(grid_view(ab), grid_view(x), mod, gpost, rowk, w_out)
    return out.reshape(b, l, d)


def _fourier_ctx_kernel(x_ref, mod_ref, gpre_ref, gpost_ref, chan_ref, seq_ref, w_ref, o_ref, *, gd):
    x = x_ref[...]
    hb = _pre(x, mod_ref, gpre_ref).astype(BF16)
    u, v = _channel_dft(hb, chan_ref, gd)
    y = _dot(seq_ref[...], jnp.concatenate([u, v], axis=0)).astype(BF16)
    o_ref[...] = _post(x, _dot(y, w_ref[...]), mod_ref, gpost_ref, 1.0)


def _fourier_ctx(x, mod, gpre, gpost, w_out):
    b, l, d = x.shape
    gd = d // FOURIER_GROUPS
    chan, _ = _fourier_consts(gd)
    c, s = _dft_cos_sin(l)
    seqm = jnp.asarray(np.concatenate([c, -s], axis=1), BF16)
    return pl.pallas_call(
        functools.partial(_fourier_ctx_kernel, gd=gd),
        grid=(b,),
        in_specs=[
            pl.BlockSpec((None, l, d), lambda i: (i, 0, 0)),
            pl.BlockSpec((None, 3, d), lambda i: (0, 0, 0)),
            _const_spec((1, d)),
            _const_spec((1, d)),
            _const_spec(chan.shape),
            _const_spec(seqm.shape),
            _const_spec(w_out.shape),
        ],
        out_specs=pl.BlockSpec((None, l, d), lambda i: (i, 0, 0)),
        out_shape=jax.ShapeDtypeStruct(x.shape, F32),
        compiler_params=_params("parallel"),
        name="fourier_ctx",
    )(x, mod, gpre, gpost, chan, seqm, w_out)


def _lower_bound(logits_ref, j):
    lg = logits_ref[...]
    e = jnp.exp(lg - jnp.max(lg, axis=0, keepdims=True))
    p = e / jnp.sum(e, axis=0, keepdims=True)
    lb = jnp.zeros_like(p[0:1])
    for i in range(1, j + 1):
        lb = lb + p[i:i + 1]
    return lb


def _forget_gate(z, lb):
    e = jnp.exp(-jnp.abs(z))
    inv = 1.0 / (1.0 + e)
    pos = z >= 0.0
    sig = jnp.where(pos, inv, e * inv)
    nsig = jnp.where(pos, e * inv, inv)
    f = jnp.maximum(lb, LB_FLOOR) + (1.0 - lb) * sig
    return (1.0 - lb) * nsig, jnp.log2(f)


def _split2(x):
    hi = x.astype(BF16)
    lo = (x - hi.astype(F32)).astype(BF16)
    return hi, lo


def _tile_cumsum(g, tri_ref):
    t = tri_ref.shape[0]
    parts = _split2(g)
    tiles = []
    for i in range(g.shape[0] // t):
        tiles.append(sum(_dot(tri_ref[...], p[i * t:(i + 1) * t]) for p in parts))
    return jnp.concatenate(tiles, axis=0)


def _hgrn_in_kernel(x_ref, mod_ref, gpre_ref, w_ref, lbf_ref, lbb_ref, trif_ref, trib_ref,
                    q_ref, v_ref, kf_ref, bf_ref, kb_ref, bb_ref, sg_ref, *, layer_j, kd, hd):
    hb = _pre(x_ref[...], mod_ref, gpre_ref).astype(BF16)
    col = lambda n: _dot(hb, w_ref[:, n * kd:(n + 1) * kd])
    q = _silu(col(0))
    v = col(1)
    kf, lf = _forget_gate(col(2), _lower_bound(lbf_ref, layer_j))
    kb, lbw = _forget_gate(col(3), _lower_bound(lbb_ref, layer_j))
    sg_ref[...] = _silu(col(4)).astype(BF16)
    bf = _tile_cumsum(lf, trif_ref)
    bb = _tile_cumsum(lbw, trib_ref)
    for h in range(HGRN_HEADS):
        sl = slice(h * hd, (h + 1) * hd)
        q_ref[h] = q[:, sl].astype(BF16)
        v_ref[h] = v[:, sl].astype(BF16)
        kf_ref[h] = kf[:, sl].astype(BF16)
        kb_ref[h] = kb[:, sl].astype(BF16)
        bf_ref[h] = bf[:, sl]
        bb_ref[h] = bb[:, sl]


def _hgrn_inputs(x, mod, gpre, w_in, lb_fwd, lb_bwd, layer_j):
    b, l, d = x.shape
    kd = d
    hd = kd // HGRN_HEADS
    tm = _token_tile(l, 512)
    t = _token_tile(tm, SCAN_TILE)
    trif, _ = _scan_consts(t, False)
    trib, _ = _scan_consts(t, True)
    x_spec = pl.BlockSpec((None, tm, d), lambda i, j: (i, j, 0))
    head_spec = pl.BlockSpec((None, HGRN_HEADS, tm, hd), lambda i, j: (i, 0, j, 0))
    heads = lambda dt: jax.ShapeDtypeStruct((b, HGRN_HEADS, l, hd), dt)
    return pl.pallas_call(
        functools.partial(_hgrn_in_kernel, layer_j=layer_j, kd=kd, hd=hd),
        grid=(b, l // tm),
        in_specs=[
            x_spec,
            pl.BlockSpec((None, 3, d), lambda i, j: (i, 0, 0)),
            _const_spec((1, d)),
            _const_spec(w_in.shape),
            _const_spec(lb_fwd.shape),
            _const_spec(lb_bwd.shape),
            _const_spec((t, t)),
            _const_spec((t, t)),
        ],
        out_specs=[head_spec] * 6 + [x_spec],
        out_shape=[heads(BF16), heads(BF16), heads(BF16), heads(F32), heads(BF16), heads(F32),
                   jax.ShapeDtypeStruct((b, l, d), BF16)],
        compiler_params=_params("parallel", "parallel"),
        name="hgrn_in",
    )(x, mod, gpre, w_in, lb_fwd, lb_bwd, trif, trib)


def _scan_consts(n, reverse):
    r = np.arange(n)[:, None]
    c = np.arange(n)[None, :]
    valid = (c >= r) if reverse else (c <= r)
    x = r ^ c
    lvl = np.where(x == 0, 0, np.floor(np.log2(np.maximum(x, 1))).astype(np.int64) + 1)
    fine = np.where(valid & (x < SUBLANES), lvl, -1)
    return jnp.asarray(valid, BF16), jnp.asarray(fine, jnp.int32)


def _level_ref(b, m, reverse):
    t, n = b.shape
    first = m if reverse else m - 1
    b3 = b.reshape(t // SUBLANES, SUBLANES, n)
    rid = lax.broadcasted_iota(jnp.int32, b3.shape, 1)
    ref = None
    for p in range(0, SUBLANES, 2 * m):
        row = jnp.broadcast_to(b3[:, p + first:p + first + 1, :], b3.shape)
        ref = row if ref is None else jnp.where(rid >= p, row, ref)
    return ref.reshape(t, n)


def _neg_abs(d):
    bits = lax.bitcast_convert_type(d, jnp.int32) | jnp.int32(-2 ** 31)
    return lax.bitcast_convert_type(bits, F32)


def _scaled(z, e):
    return (z * e).astype(BF16)


def _halves(lo, m, reverse):
    return (lo, lo + m, lo + m) if reverse else (lo + m, lo, lo + m - 1)


def _strip_scores(q, k, q32, k32, b, fine, lane, reverse):
    n = q.shape[0]
    a = jnp.where(fine == 0, _dot_nt(q, k), 0.0)
    before = pltpu.roll(b, (n - 1) if reverse else 1, 0)
    a = jnp.where(fine == 1, _dot_nt(_scaled(q32, jnp.exp2(_neg_abs(b - before))), k), a)
    m, level = 2, 2
    while m < SUBLANES:
        e = jnp.exp2(_neg_abs(b - _level_ref(b, m, reverse)))
        a = jnp.where(fine == level, _dot_nt(_scaled(q32, e), _scaled(k32, e)), a)
        m, level = 2 * m, level + 1
    blocks = [a[i:i + SUBLANES] for i in range(0, n, SUBLANES)]
    while m < n:
        ql, kf = [], []
        for lo in range(0, n, 2 * m):
            late, early, first = _halves(lo, m, reverse)
            ref = b[first:first + 1, :]
            ql.append(_scaled(q32[late:late + m], jnp.exp2(b[late:late + m] - ref)))
            ke = _scaled(k32[early:early + m], jnp.exp2(ref - b[early:early + m]))
            kf += [k[late:late + m], ke] if reverse else [ke, k[late:late + m]]
        s_m = _dot_nt(jnp.concatenate(ql, axis=0), jnp.concatenate(kf, axis=0))
        for pi, lo in enumerate(range(0, n, 2 * m)):
            late, early, _ = _halves(lo, m, reverse)
            inside = (lane >= early) & (lane < early + m)
            for i in range(0, m, SUBLANES):
                rb = (late + i) // SUBLANES
                blocks[rb] = jnp.where(inside, s_m[pi * m + i:pi * m + i + SUBLANES], blocks[rb])
        m *= 2
    return jnp.concatenate(blocks, axis=0)


def _scan_tile(q, k, v, b, st, fine, reverse):
    t = q.shape[0]
    n = fine.shape[0]
    q32 = q.astype(F32)
    k32 = k.astype(F32)
    end = 0 if reverse else t - 1
    b_end = b[end:end + 1, :]

    o = _dot_nt(_scaled(q32, jnp.exp2(b)), st.astype(BF16))
    new_st = st * jnp.exp2(b_end) + _dot_tn(v, _scaled(k32, jnp.exp2(b_end - b)))

    lane = lax.broadcasted_iota(jnp.int32, (SUBLANES, n), 1)
    strips = range(0, t, n)
    score = {(i, i): _strip_scores(q[i:i + n], k[i:i + n], q32[i:i + n], k32[i:i + n], b[i:i + n],
                                   fine, lane, reverse) for i in strips}
    m = n
    while m < t:
        for lo in range(0, t, 2 * m):
            late, early, first = _halves(lo, m, reverse)
            ref = b[first:first + 1, :]
            ql = _scaled(q32[late:late + m], jnp.exp2(b[late:late + m] - ref))
            ke = _scaled(k32[early:early + m], jnp.exp2(ref - b[early:early + m]))
            s_m = _dot_nt(ql, ke)
            for i in range(0, m, n):
                for j in range(0, m, n):
                    score[(late + i, early + j)] = s_m[i:i + n, j:j + n]
        m *= 2
    outs = []
    for i in strips:
        keys = [j for j in strips if (i, j) in score]
        a = jnp.concatenate([score[(i, j)] for j in keys], axis=1).astype(BF16)
        outs.append(_dot(a, jnp.concatenate([v[j:j + n] for j in keys], axis=0)))
    return o + jnp.concatenate(outs, axis=0), new_st


def _scan_kernel(q_ref, k_ref, v_ref, b_ref, s0_ref, fine_ref, o_ref, sfin_ref, state_ref, *, reverse, n_tiles):
    step = pl.program_id(2)

    @pl.when(step == 0)
    def _():
        state_ref[...] = s0_ref[...]

    fine = fine_ref[...]
    for h in range(q_ref.shape[0]):
        o, st = _scan_tile(q_ref[h], k_ref[h], v_ref[h], b_ref[h], state_ref[h], fine, reverse)
        state_ref[h] = st
        o_ref[h] = o.astype(o_ref.dtype)

    @pl.when(step == n_tiles - 1)
    def _():
        sfin_ref[...] = state_ref[...]


def _scan(q, k, v, bcum, s0, reverse):
    b, h, l, hd = q.shape
    t = _token_tile(l, SCAN_TILE)
    n_tiles = l // t
    hp = SCAN_HEADS
    _, fine = _scan_consts(LANES, reverse)
    tile = (lambda j: n_tiles - 1 - j) if reverse else (lambda j: j)
    tok_spec = pl.BlockSpec((None, hp, t, hd), lambda i, hh, j: (i, hh, tile(j), 0))
    st_spec = pl.BlockSpec((None, hp, hd, hd), lambda i, hh, j: (i, hh, 0, 0))
    return pl.pallas_call(
        functools.partial(_scan_kernel, reverse=reverse, n_tiles=n_tiles),
        grid=(b, h // hp, n_tiles),
        in_specs=[tok_spec, tok_spec, tok_spec, tok_spec, st_spec, _const_spec(fine.shape)],
        out_specs=[tok_spec, st_spec],
        out_shape=[jax.ShapeDtypeStruct((b, h, l, hd), BF16), jax.ShapeDtypeStruct((b, h, hd, hd), F32)],
        scratch_shapes=[pltpu.VMEM((hp, hd, hd), F32)],
        compiler_params=_params("parallel", "parallel", "arbitrary"),
        name="scan_bwd" if reverse else "scan_fwd",
    )(q, k, v, bcum, s0, fine)


def _hgrn_out_kernel(of_ref, ob_ref, sg_ref, x_ref, mod_ref, gpost_ref, gn_ref, w_ref, o_ref):
    heads = []
    for h in range(HGRN_HEADS):
        heads.append(_rms(of_ref[h].astype(F32) + ob_ref[h].astype(F32), gn_ref[...]))
    o = jnp.concatenate(heads, axis=1)
    y = _dot((o * sg_ref[...].astype(F32)).astype(BF16), w_ref[...])
    o_ref[...] = _post(x_ref[...], y, mod_ref, gpost_ref, 1.0)


def _hgrn_readout(o_f, o_b, sg, x, mod, gpost, g_norm, w_out):
    b, l, d = x.shape
    hd = d // HGRN_HEADS
    tm = _token_tile(l, 512)
    x_spec = pl.BlockSpec((None, tm, d), lambda i, j: (i, j, 0))
    head_spec = pl.BlockSpec((None, HGRN_HEADS, tm, hd), lambda i, j: (i, 0, j, 0))
    return pl.pallas_call(
        _hgrn_out_kernel,
        grid=(b, l // tm),
        in_specs=[
            head_spec, head_spec, x_spec, x_spec,
            pl.BlockSpec((None, 3, d), lambda i, j: (i, 0, 0)),
            _const_spec((1, d)),
            _const_spec((1, hd)),
            _const_spec(w_out.shape),
        ],
        out_specs=x_spec,
        out_shape=jax.ShapeDtypeStruct(x.shape, F32),
        compiler_params=_params("parallel", "parallel"),
        name="hgrn_out",
    )(o_f, o_b, sg, x, mod, gpost, g_norm, w_out)


def kernel(x, c, ctx, c_ctx, ada_w, ada_b, norm_pre, norm_post, ffn_w_in, ffn_w_out, fourier_w_out,
           hgrn_w_in, hgrn_lb_fwd, hgrn_lb_bwd, hgrn_norm, hgrn_w_out):
    bsz, _, d = x.shape
    depth = ada_w.shape[0]
    assert bsz + 1 <= MOD_ROWS

    c_rows = jnp.concatenate([c, c_ctx[None, :], jnp.zeros((MOD_ROWS - bsz - 1, d), F32)], axis=0)
    mod = _adaln(c_rows, ada_w, ada_b).reshape(depth, MOD_ROWS, N_MOD, d)

    ffn_w_in = ffn_w_in.astype(BF16)
    ffn_w_out = ffn_w_out.astype(BF16)
    fourier_w_out = fourier_w_out.astype(BF16)
    hgrn_w_in = hgrn_w_in.astype(BF16)
    hgrn_w_out = hgrn_w_out.astype(BF16)

    for i in range(depth):
        last = i == depth - 1
        is_hgrn = i % 2 == 1
        jm = i // 2
        mx = lambda j: mod[i, :bsz, 3 * j:3 * j + 3]
        mc = lambda j: mod[i, bsz:bsz + 1, 3 * j:3 * j + 3]
        gpre = lambda j: norm_pre[i, j][None, :]
        gpost = lambda j: norm_post[i, j][None, :]
        ctx_mod = lambda j: jnp.broadcast_to(mc(j), (bsz, 3, d))

        x = _ffn(x, mx(0), gpre(0), gpost(0), ffn_w_in[i, 0], ffn_w_out[i, 0])
        if is_hgrn or not last:
            ctx = _ffn(ctx, ctx_mod(0), gpre(0), gpost(0), ffn_w_in[i, 0], ffn_w_out[i, 0])

        if not is_hgrn:
            x = _fourier_latent(x, mx(1), gpre(1), gpost(1), fourier_w_out[jm])
            if not last:
                ctx = _fourier_ctx(ctx, mc(1), gpre(1), gpost(1), fourier_w_out[jm])
        else:
            hd = d // HGRN_HEADS
            zero = jnp.zeros((bsz, HGRN_HEADS, hd, hd), F32)
            qc, vc, kfc, bfc, kbc, bbc, sgc = _hgrn_inputs(
                ctx, ctx_mod(1), gpre(1), hgrn_w_in[jm], hgrn_lb_fwd, hgrn_lb_bwd, jm)
            ocf, s_f = _scan(qc, kfc, vc, bfc, zero, False)
            ocb, s_b = _scan(qc, kbc, vc, bbc, zero, True)
            qx, vx, kfx, bfx, kbx, bbx, sgx = _hgrn_inputs(
                x, mx(1), gpre(1), hgrn_w_in[jm], hgrn_lb_fwd, hgrn_lb_bwd, jm)
            oxf, _ = _scan(qx, kfx, vx, bfx, s_f, False)
            oxb, _ = _scan(qx, kbx, vx, bbx, s_b, True)
            gn = hgrn_norm[jm][None, :]
            x = _hgrn_readout(oxf, oxb, sgx, x, mx(1), gpost(1), gn, hgrn_w_out[jm])
            if not last:
                ctx = _hgrn_readout(ocf, ocb, sgc, ctx, ctx_mod(1), gpost(1), gn, hgrn_w_out[jm])

        x = _ffn(x, mx(2), gpre(2), gpost(2), ffn_w_in[i, 1], ffn_w_out[i, 1])
        if not last:
            ctx = _ffn(ctx, ctx_mod(2), gpre(2), gpost(2), ffn_w_in[i, 1], ffn_w_out[i, 1])
    return x
```

```python
import functools

import jax
import jax.numpy as jnp
import numpy as np
from jax import lax
from jax.experimental import pallas as pl
from jax.experimental.pallas import tpu as pltpu

F32 = jnp.float32
BF16 = jnp.bfloat16

GRID_W = 64
FOURIER_GROUPS = 4
HGRN_HEADS = 8
N_MOD = 9
HALF = 0.5
NORM_EPS = 1e-6
LB_FLOOR = 1e-30

VMEM_LIMIT_V7X = 56 * 1024 * 1024
SUBLANES = 8
LANES = 128
MXU_COLS = 256
SCAN_TILE = 512
SCAN_HEADS = 8
MOD_ROWS = 16


def _params(*sem):
    return pltpu.CompilerParams(dimension_semantics=sem, vmem_limit_bytes=VMEM_LIMIT_V7X)


def _const_spec(shape):
    n = len(shape)
    return pl.BlockSpec(shape, lambda *_: (0,) * n, pipeline_mode=pl.Buffered(1))


def _pick_spec(stacked, lead):
    tail = stacked.shape[len(lead):]
    return pl.BlockSpec((None,) * len(lead) + tail, lambda *_: tuple(lead) + (0,) * len(tail),
                        pipeline_mode=pl.Buffered(1))


def _rms(x, g):
    ms = jnp.mean(x * x, axis=-1, keepdims=True)
    return x * lax.rsqrt(ms + NORM_EPS) * g


def _pre(x, mod_ref, g_ref):
    return _rms(x, g_ref[...]) * (1.0 + mod_ref[1:2, :]) + mod_ref[0:1, :]


def _post(x, y, mod_ref, g_ref, w):
    return x + w * mod_ref[2:3, :] * _rms(y, g_ref[...])


def _silu(x):
    return x * jax.nn.sigmoid(x)


def _dot(a, b):
    return jnp.dot(a, b, preferred_element_type=F32)


def _dot_nt(a, b):
    return lax.dot_general(a, b, (((1,), (1,)), ((), ())), preferred_element_type=F32)


def _dot_tn(a, b):
    return lax.dot_general(a, b, (((0,), (0,)), ((), ())), preferred_element_type=F32)


def _adaln_kernel(c_ref, w_ref, b_ref, o_ref):
    sc = _silu(c_ref[...]).astype(BF16)
    o_ref[...] = _dot(sc, w_ref[...].astype(BF16)) + b_ref[...]


def _adaln(c_rows, ada_w, ada_b):
    depth, d, n = ada_w.shape
    tn = 1024
    return pl.pallas_call(
        _adaln_kernel,
        grid=(depth, n // tn),
        in_specs=[
            pl.BlockSpec((MOD_ROWS, d), lambda i, j: (0, 0)),
            pl.BlockSpec((None, d, tn), lambda i, j: (i, 0, j)),
            pl.BlockSpec((None, 1, tn), lambda i, j: (i, 0, j)),
        ],
        out_specs=pl.BlockSpec((None, MOD_ROWS, tn), lambda i, j: (i, 0, j)),
        out_shape=jax.ShapeDtypeStruct((depth, MOD_ROWS, n), F32),
        compiler_params=_params("parallel", "parallel"),
        name="adaln",
    )(c_rows, ada_w, ada_b.reshape(depth, 1, n))


def _ffn_kernel(x_ref, mod_ref, gpre_ref, gpost_ref, win_ref, wout_ref, o_ref, *, d_ff):
    x = x_ref[...]
    hb = _pre(x, mod_ref, gpre_ref).astype(BF16)
    fc = MXU_COLS
    acc = None
    for ci in range(d_ff // fc):
        gate = _dot(hb, win_ref[:, ci * fc:(ci + 1) * fc])
        up = _dot(hb, win_ref[:, d_ff + ci * fc:d_ff + (ci + 1) * fc])
        a = (_silu(gate) * up).astype(BF16)
        y = _dot(a, wout_ref[ci * fc:(ci + 1) * fc, :])
        acc = y if acc is None else acc + y
    o_ref[...] = _post(x, acc, mod_ref, gpost_ref, HALF)


def _token_tile(n_tokens, want):
    return want if n_tokens % want == 0 else n_tokens


def _ffn(s, mod, gpre, gpost, w_in, w_out, lead):
    b, l, d = s.shape
    d_ff = w_out.shape[-2]
    assert d_ff % MXU_COLS == 0
    tm = _token_tile(l, 1024)
    return pl.pallas_call(
        functools.partial(_ffn_kernel, d_ff=d_ff),
        grid=(b, l // tm),
        in_specs=[
            pl.BlockSpec((None, tm, d), lambda i, j: (i, j, 0)),
            pl.BlockSpec((None, 3, d), lambda i, j: (i, 0, 0)),
            _const_spec((1, d)),
            _const_spec((1, d)),
            _pick_spec(w_in, lead),
            _pick_spec(w_out, lead),
        ],
        out_specs=pl.BlockSpec((None, tm, d), lambda i, j: (i, j, 0)),
        out_shape=jax.ShapeDtypeStruct(s.shape, F32),
        compiler_params=_params("parallel", "parallel"),
        name="ffn",
    )(s, mod, gpre, gpost, w_in, w_out)


def _dft_cos_sin(n):
    k = np.arange(n)
    ang = 2.0 * np.pi * ((k[:, None] * k[None, :]) % n) / n
    s = 1.0 / np.sqrt(n)
    return np.cos(ang) * s, np.sin(ang) * s


def _fourier_consts(gd):
    c, s = _dft_cos_sin(gd)
    chan = np.concatenate([c, s], axis=1)
    c, s = _dft_cos_sin(GRID_W)
    col = np.block([[c, -s], [s, c]])
    return jnp.asarray(chan, BF16), jnp.asarray(col, BF16)


def _channel_dft(hb, chan_ref, gd):
    us, vs = [], []
    for g in range(FOURIER_GROUPS):
        uv = _dot(hb[:, g * gd:(g + 1) * gd], chan_ref[...])
        us.append(uv[:, :gd])
        vs.append(uv[:, gd:])
    return (jnp.concatenate(us, axis=1).astype(BF16), jnp.concatenate(vs, axis=1).astype(BF16))


def _pack_pair(a, b):
    hi = lax.bitcast_convert_type(a.astype(BF16).astype(F32), jnp.uint32)
    lo = lax.bitcast_convert_type(b.astype(BF16).astype(F32), jnp.uint32)
    return hi | (lo >> 16)


def _unpack_pair(p):
    a = lax.bitcast_convert_type(p & jnp.uint32(0xFFFF0000), F32)
    b = lax.bitcast_convert_type(p << 16, F32)
    return a.astype(BF16), b.astype(BF16)


def _fourier_cols_kernel(x_ref, mod_ref, gpre_ref, chan_ref, col_ref, ab_ref, *, rows_per_step, gd):
    hb = _pre(x_ref[...], mod_ref, gpre_ref).astype(BF16)
    u, v = _channel_dft(hb, chan_ref, gd)
    for r in range(rows_per_step):
        sl = slice(r * GRID_W, (r + 1) * GRID_W)
        ab = _dot(col_ref[...], jnp.concatenate([u[sl], v[sl]], axis=0))
        ab_ref[sl, :] = _pack_pair(ab[:GRID_W], ab[GRID_W:])


def _fourier_rows_kernel(ab_ref, x_ref, mod_ref, gpost_ref, rowk_ref, w_ref, o_ref):
    rows, cps, d = x_ref.shape
    flat = lambda v: v.reshape(rows * cps, d)
    a, b = _unpack_pair(flat(ab_ref[...]))
    y = _dot(rowk_ref[...], jnp.concatenate([a, b], axis=0)).astype(BF16)
    res = _post(flat(x_ref[...]), _dot(y, w_ref[...]), mod_ref, gpost_ref, 1.0)
    o_ref[...] = res.reshape(rows, cps, d)


def _fourier_latent(x, mod, gpre, gpost, w_out, lead):
    b, l, d = x.shape
    rows = l // GRID_W
    gd = d // FOURIER_GROUPS
    chan, col = _fourier_consts(gd)
    rps = 8 if rows % 8 == 0 else 1
    tm = rps * GRID_W
    ab = pl.pallas_call(
        functools.partial(_fourier_cols_kernel, rows_per_step=rps, gd=gd),
        grid=(b, l // tm),
        in_specs=[
            pl.BlockSpec((None, tm, d), lambda i, j: (i, j, 0)),
            pl.BlockSpec((None, 3, d), lambda i, j: (i, 0, 0)),
            _const_spec((1, d)),
            _const_spec(chan.shape),
            _const_spec(col.shape),
        ],
        out_specs=pl.BlockSpec((None, tm, d), lambda i, j: (i, j, 0)),
        out_shape=jax.ShapeDtypeStruct((b, l, d), jnp.uint32),
        compiler_params=_params("parallel", "parallel"),
        name="fourier_cols",
    )(x, mod, gpre, chan, col)

    c, s = _dft_cos_sin(rows)
    cps = SUBLANES
    eye = np.eye(cps)
    rowk = jnp.asarray(np.concatenate([np.kron(c, eye), -np.kron(s, eye)], axis=1), BF16)
    grid_view = lambda t: t.reshape(b, rows, GRID_W, d)
    blk = pl.BlockSpec((None, rows, cps, d), lambda i, j: (i, 0, j, 0))
    out = pl.pallas_call(
        _fourier_rows_kernel,
        grid=(b, GRID_W // cps),
        in_specs=[
            blk, blk,
            pl.BlockSpec((None, 3, d), lambda i, j: (i, 0, 0)),
            _const_spec((1, d)),
            _const_spec(rowk.shape),
            _pick_spec(w_out, lead),
        ],
        out_specs=blk,
        out_shape=jax.ShapeDtypeStruct((b, rows, GRID_W, d), F32),
        compiler_params=_params("parallel", "parallel"),
        name="fourier_rows",
    )(grid_view(ab), grid_view(x), mod, gpost, rowk, w_out)
    return out.reshape(b, l, d)


def _fourier_ctx_kernel(x_ref, mod_ref, gpre_ref, gpost_ref, chan_ref, seq_ref, w_ref, o_ref, *, gd):
    x = x_ref[...]
    hb = _pre(x, mod_ref, gpre_ref).astype(BF16)
    u, v = _channel_dft(hb, chan_ref, gd)
    y = _dot(seq_ref[...], jnp.concatenate([u, v], axis=0)).astype(BF16)
    o_ref[...] = _post(x, _dot(y, w_ref[...]), mod_ref, gpost_ref, 1.0)


def _fourier_ctx(x, mod, gpre, gpost, w_out, lead):
    b, l, d = x.shape
    gd = d // FOURIER_GROUPS
    chan, _ = _fourier_consts(gd)
    c, s = _dft_cos_sin(l)
    seqm = jnp.asarray(np.concatenate([c, -s], axis=1), BF16)
    return pl.pallas_call(
        functools.partial(_fourier_ctx_kernel, gd=gd),
        grid=(b,),
        in_specs=[
            pl.BlockSpec((None, l, d), lambda i: (i, 0, 0)),
            pl.BlockSpec((None, 3, d), lambda i: (0, 0, 0)),
            _const_spec((1, d)),
            _const_spec((1, d)),
            _const_spec(chan.shape),
            _const_spec(seqm.shape),
            _pick_spec(w_out, lead),
        ],
        out_specs=pl.BlockSpec((None, l, d), lambda i: (i, 0, 0)),
        out_shape=jax.ShapeDtypeStruct(x.shape, F32),
        compiler_params=_params("parallel"),
        name="fourier_ctx",
    )(x, mod, gpre, gpost, chan, seqm, w_out)


def _lower_bound(logits_ref, j):
    lg = logits_ref[...]
    e = jnp.exp(lg - jnp.max(lg, axis=0, keepdims=True))
    p = e / jnp.sum(e, axis=0, keepdims=True)
    lb = jnp.zeros_like(p[0:1])
    for i in range(1, j + 1):
        lb = lb + p[i:i + 1]
    return lb


def _forget_gate(z, lb):
    e = jnp.exp(-jnp.abs(z))
    inv = 1.0 / (1.0 + e)
    pos = z >= 0.0
    sig = jnp.where(pos, inv, e * inv)
    nsig = jnp.where(pos, e * inv, inv)
    f = jnp.maximum(lb, LB_FLOOR) + (1.0 - lb) * sig
    return (1.0 - lb) * nsig, jnp.log2(f)


def _split2(x):
    hi = x.astype(BF16)
    lo = (x - hi.astype(F32)).astype(BF16)
    return hi, lo


def _tile_cumsum(g, tri_ref, tile, reverse):
    c = tri_ref.shape[0]
    parts = _split2(g)
    chunks = [sum(_dot(tri_ref[...], p[i:i + c]) for p in parts) for i in range(0, g.shape[0], c)]
    per_tile = tile // c
    order = range(per_tile - 1, -1, -1) if reverse else range(per_tile)
    last = 0 if reverse else c - 1
    for t0 in range(0, len(chunks), per_tile):
        carry = None
        for i in order:
            if carry is not None:
                chunks[t0 + i] = chunks[t0 + i] + carry
            carry = chunks[t0 + i][last:last + 1, :]
    return jnp.concatenate(chunks, axis=0)


def _hgrn_in_kernel(x_ref, mod_ref, gpre_ref, w_ref, lbf_ref, lbb_ref, trif_ref, trib_ref,
                    q_ref, v_ref, kf_ref, bf_ref, kb_ref, bb_ref, sg_ref, *, layer_j, kd, hd, tile):
    hb = _pre(x_ref[...], mod_ref, gpre_ref).astype(BF16)
    col = lambda n: _dot(hb, w_ref[:, n * kd:(n + 1) * kd])
    q = _silu(col(0))
    v = col(1)
    kf, lf = _forget_gate(col(2), _lower_bound(lbf_ref, layer_j))
    kb, lbw = _forget_gate(col(3), _lower_bound(lbb_ref, layer_j))
    sg_ref[...] = _silu(col(4)).astype(BF16)
    bf = _tile_cumsum(lf, trif_ref, tile, False)
    bb = _tile_cumsum(lbw, trib_ref, tile, True)
    for h in range(HGRN_HEADS):
        sl = slice(h * hd, (h + 1) * hd)
        q_ref[h] = q[:, sl].astype(BF16)
        v_ref[h] = v[:, sl].astype(BF16)
        kf_ref[h] = kf[:, sl].astype(BF16)
        kb_ref[h] = kb[:, sl].astype(BF16)
        bf_ref[h] = bf[:, sl]
        bb_ref[h] = bb[:, sl]


def _hgrn_inputs(x, mod, gpre, w_in, lead, lb_fwd, lb_bwd, layer_j):
    b, l, d = x.shape
    kd = d
    hd = kd // HGRN_HEADS
    tm = _token_tile(l, 512)
    tile = _token_tile(tm, SCAN_TILE)
    t = _token_tile(tile, MXU_COLS)
    trif, _ = _scan_consts(t, False)
    trib, _ = _scan_consts(t, True)
    x_spec = pl.BlockSpec((None, tm, d), lambda i, j: (i, j, 0))
    head_spec = pl.BlockSpec((None, HGRN_HEADS, tm, hd), lambda i, j: (i, 0, j, 0))
    heads = lambda dt: jax.ShapeDtypeStruct((b, HGRN_HEADS, l, hd), dt)
    return pl.pallas_call(
        functools.partial(_hgrn_in_kernel, layer_j=layer_j, kd=kd, hd=hd, tile=tile),
        grid=(b, l // tm),
        in_specs=[
            x_spec,
            pl.BlockSpec((None, 3, d), lambda i, j: (i, 0, 0)),
            _const_spec((1, d)),
            _pick_spec(w_in, lead),
            _const_spec(lb_fwd.shape),
            _const_spec(lb_bwd.shape),
            _const_spec((t, t)),
            _const_spec((t, t)),
        ],
        out_specs=[head_spec] * 6 + [x_spec],
        out_shape=[heads(BF16), heads(BF16), heads(BF16), heads(F32), heads(BF16), heads(F32),
                   jax.ShapeDtypeStruct((b, l, d), BF16)],
        compiler_params=_params("parallel", "parallel"),
        name="hgrn_in",
    )(x, mod, gpre, w_in, lb_fwd, lb_bwd, trif, trib)


def _scan_consts(n, reverse):
    r = np.arange(n)[:, None]
    c = np.arange(n)[None, :]
    valid = (c >= r) if reverse else (c <= r)
    x = r ^ c
    lvl = np.where(x == 0, 0, np.floor(np.log2(np.maximum(x, 1))).astype(np.int64) + 1)
    fine = np.where(valid & (x < SUBLANES), lvl, -1)
    return jnp.asarray(valid, BF16), jnp.asarray(fine, jnp.int32)


def _level_ref(b, m, reverse):
    t, n = b.shape
    first = m if reverse else m - 1
    b3 = b.reshape(t // SUBLANES, SUBLANES, n)
    rid = lax.broadcasted_iota(jnp.int32, b3.shape, 1)
    ref = None
    for p in range(0, SUBLANES, 2 * m):
        row = jnp.broadcast_to(b3[:, p + first:p + first + 1, :], b3.shape)
        ref = row if ref is None else jnp.where(rid >= p, row, ref)
    return ref.reshape(t, n)


def _neg_abs(d):
    bits = lax.bitcast_convert_type(d, jnp.int32) | jnp.int32(-2 ** 31)
    return lax.bitcast_convert_type(bits, F32)


def _scaled(z, e):
    return (z * e).astype(BF16)


def _halves(lo, m, reverse):
    return (lo, lo + m, lo + m) if reverse else (lo + m, lo, lo + m - 1)


def _strip_scores(q, k, q32, k32, b, fine, lane, reverse):
    n = q.shape[0]
    a = jnp.where(fine == 0, _dot_nt(q, k), 0.0)
    before = pltpu.roll(b, (n - 1) if reverse else 1, 0)
    a = jnp.where(fine == 1, _dot_nt(_scaled(q32, jnp.exp2(_neg_abs(b - before))), k), a)
    m, level = 2, 2
    while m < SUBLANES:
        e = jnp.exp2(_neg_abs(b - _level_ref(b, m, reverse)))
        a = jnp.where(fine == level, _dot_nt(_scaled(q32, e), _scaled(k32, e)), a)
        m, level = 2 * m, level + 1
    blocks = [a[i:i + SUBLANES] for i in range(0, n, SUBLANES)]
    while m < n:
        ql, kf = [], []
        for lo in range(0, n, 2 * m):
            late, early, first = _halves(lo, m, reverse)
            ref = b[first:first + 1, :]
            ql.append(_scaled(q32[late:late + m], jnp.exp2(b[late:late + m] - ref)))
            ke = _scaled(k32[early:early + m], jnp.exp2(ref - b[early:early + m]))
            kf += [k[late:late + m], ke] if reverse else [ke, k[late:late + m]]
        s_m = _dot_nt(jnp.concatenate(ql, axis=0), jnp.concatenate(kf, axis=0))
        for pi, lo in enumerate(range(0, n, 2 * m)):
            late, early, _ = _halves(lo, m, reverse)
            inside = (lane >= early) & (lane < early + m)
            for i in range(0, m, SUBLANES):
                rb = (late + i) // SUBLANES
                blocks[rb] = jnp.where(inside, s_m[pi * m + i:pi * m + i + SUBLANES], blocks[rb])
        m *= 2
    return jnp.concatenate(blocks, axis=0)


def _scan_tile(q, k, v, b, st, fine, reverse):
    t = q.shape[0]
    n = fine.shape[0]
    q32 = q.astype(F32)
    k32 = k.astype(F32)
    end = 0 if reverse else t - 1
    b_end = b[end:end + 1, :]

    o = _dot_nt(_scaled(q32, jnp.exp2(b)), st.astype(BF16))
    new_st = st * jnp.exp2(b_end) + _dot_tn(v, _scaled(k32, jnp.exp2(b_end - b)))

    lane = lax.broadcasted_iota(jnp.int32, (SUBLANES, n), 1)
    strips = range(0, t, n)
    score = {(i, i): _strip_scores(q[i:i + n], k[i:i + n], q32[i:i + n], k32[i:i + n], b[i:i + n],
                                   fine, lane, reverse) for i in strips}
    m = n
    while m < t:
        for lo in range(0, t, 2 * m):
            late, early, first = _halves(lo, m, reverse)
            ref = b[first:first + 1, :]
            ql = _scaled(q32[late:late + m], jnp.exp2(b[late:late + m] - ref))
            ke = _scaled(k32[early:early + m], jnp.exp2(ref - b[early:early + m]))
            s_m = _dot_nt(ql, ke)
            for i in range(0, m, n):
                for j in range(0, m, n):
                    score[(late + i, early + j)] = s_m[i:i + n, j:j + n]
        m *= 2
    outs = []
    for i in strips:
        keys = [j for j in strips if (i, j) in score]
        a = jnp.concatenate([score[(i, j)] for j in keys], axis=1).astype(BF16)
        outs.append(_dot(a, jnp.concatenate([v[j:j + n] for j in keys], axis=0)))
    return o + jnp.concatenate(outs, axis=0), new_st


def _scan_kernel(q_ref, k_ref, v_ref, b_ref, s0_ref, fine_ref, o_ref, sfin_ref, state_ref, *, reverse, n_tiles):
    step = pl.program_id(2)

    @pl.when(step == 0)
    def _():
        state_ref[...] = s0_ref[...]

    fine = fine_ref[...]
    for h in range(q_ref.shape[0]):
        o, st = _scan_tile(q_ref[h], k_ref[h], v_ref[h], b_ref[h], state_ref[h], fine, reverse)
        state_ref[h] = st
        o_ref[h] = o.astype(o_ref.dtype)

    @pl.when(step == n_tiles - 1)
    def _():
        sfin_ref[...] = state_ref[...]


def _scan(q, k, v, bcum, s0, reverse):
    b, h, l, hd = q.shape
    t = _token_tile(l, SCAN_TILE)
    n_tiles = l // t
    hp = SCAN_HEADS
    _, fine = _scan_consts(LANES, reverse)
    tile = (lambda j: n_tiles - 1 - j) if reverse else (lambda j: j)
    tok_spec = pl.BlockSpec((None, hp, t, hd), lambda i, hh, j: (i, hh, tile(j), 0))
    st_spec = pl.BlockSpec((None, hp, hd, hd), lambda i, hh, j: (i, hh, 0, 0))
    return pl.pallas_call(
        functools.partial(_scan_kernel, reverse=reverse, n_tiles=n_tiles),
        grid=(b, h // hp, n_tiles),
        in_specs=[tok_spec, tok_spec, tok_spec, tok_spec, st_spec, _const_spec(fine.shape)],
        out_specs=[tok_spec, st_spec],
        out_shape=[jax.ShapeDtypeStruct((b, h, l, hd), BF16), jax.ShapeDtypeStruct((b, h, hd, hd), F32)],
        scratch_shapes=[pltpu.VMEM((hp, hd, hd), F32)],
        compiler_params=_params("parallel", "parallel", "arbitrary"),
        name="scan_bwd" if reverse else "scan_fwd",
    )(q, k, v, bcum, s0, fine)


def _hgrn_out_kernel(of_ref, ob_ref, sg_ref, x_ref, mod_ref, gpost_ref, gn_ref, w_ref, o_ref):
    heads = []
    for h in range(HGRN_HEADS):
        heads.append(_rms(of_ref[h].astype(F32) + ob_ref[h].astype(F32), gn_ref[...]))
    o = jnp.concatenate(heads, axis=1)
    y = _dot((o * sg_ref[...].astype(F32)).astype(BF16), w_ref[...])
    o_ref[...] = _post(x_ref[...], y, mod_ref, gpost_ref, 1.0)


def _hgrn_readout(o_f, o_b, sg, x, mod, gpost, g_norm, w_out, lead):
    b, l, d = x.shape
    hd = d // HGRN_HEADS
    tm = _token_tile(l, 512)
    x_spec = pl.BlockSpec((None, tm, d), lambda i, j: (i, j, 0))
    head_spec = pl.BlockSpec((None, HGRN_HEADS, tm, hd), lambda i, j: (i, 0, j, 0))
    return pl.pallas_call(
        _hgrn_out_kernel,
        grid=(b, l // tm),
        in_specs=[
            head_spec, head_spec, x_spec, x_spec,
            pl.BlockSpec((None, 3, d), lambda i, j: (i, 0, 0)),
            _const_spec((1, d)),
            _const_spec((1, hd)),
            _pick_spec(w_out, lead),
        ],
        out_specs=x_spec,
        out_shape=jax.ShapeDtypeStruct(x.shape, F32),
        compiler_params=_params("parallel", "parallel"),
        name="hgrn_out",
    )(o_f, o_b, sg, x, mod, gpost, g_norm, w_out)


def kernel(x, c, ctx, c_ctx, ada_w, ada_b, norm_pre, norm_post, ffn_w_in, ffn_w_out, fourier_w_out,
           hgrn_w_in, hgrn_lb_fwd, hgrn_lb_bwd, hgrn_norm, hgrn_w_out):
    bsz, _, d = x.shape
    depth = ada_w.shape[0]
    assert bsz + 1 <= MOD_ROWS

    c_rows = jnp.concatenate([c, c_ctx[None, :], jnp.zeros((MOD_ROWS - bsz - 1, d), F32)], axis=0)
    mod = _adaln(c_rows, ada_w, ada_b).reshape(depth, MOD_ROWS, N_MOD, d)

    ffn_w_in = ffn_w_in.astype(BF16)
    ffn_w_out = ffn_w_out.astype(BF16)
    fourier_w_out = fourier_w_out.astype(BF16)
    hgrn_w_in = hgrn_w_in.astype(BF16)
    hgrn_w_out = hgrn_w_out.astype(BF16)

    for i in range(depth):
        last = i == depth - 1
        is_hgrn = i % 2 == 1
        jm = i // 2
        mx = lambda j: mod[i, :bsz, 3 * j:3 * j + 3]
        mc = lambda j: mod[i, bsz:bsz + 1, 3 * j:3 * j + 3]
        gpre = lambda j: norm_pre[i, j][None, :]
        gpost = lambda j: norm_post[i, j][None, :]
        ctx_mod = lambda j: jnp.broadcast_to(mc(j), (bsz, 3, d))

        x = _ffn(x, mx(0), gpre(0), gpost(0), ffn_w_in, ffn_w_out, (i, 0))
        if is_hgrn or not last:
            ctx = _ffn(ctx, ctx_mod(0), gpre(0), gpost(0), ffn_w_in, ffn_w_out, (i, 0))

        if not is_hgrn:
            x = _fourier_latent(x, mx(1), gpre(1), gpost(1), fourier_w_out, (jm,))
            if not last:
                ctx = _fourier_ctx(ctx, mc(1), gpre(1), gpost(1), fourier_w_out, (jm,))
        else:
            hd = d // HGRN_HEADS
            zero = jnp.zeros((bsz, HGRN_HEADS, hd, hd), F32)
            qc, vc, kfc, bfc, kbc, bbc, sgc = _hgrn_inputs(
                ctx, ctx_mod(1), gpre(1), hgrn_w_in, (jm,), hgrn_lb_fwd, hgrn_lb_bwd, jm)
            ocf, s_f = _scan(qc, kfc, vc, bfc, zero, False)
            ocb, s_b = _scan(qc, kbc, vc, bbc, zero, True)
            qx, vx, kfx, bfx, kbx, bbx, sgx = _hgrn_inputs(
                x, mx(1), gpre(1), hgrn_w_in, (jm,), hgrn_lb_fwd, hgrn_lb_bwd, jm)
            oxf, _ = _scan(qx, kfx, vx, bfx, s_f, False)
            oxb, _ = _scan(qx, kbx, vx, bbx, s_b, True)
            gn = hgrn_norm[jm][None, :]
            x = _hgrn_readout(oxf, oxb, sgx, x, mx(1), gpost(1), gn, hgrn_w_out, (jm,))
            if not last:
                ctx = _hgrn_readout(ocf, ocb, sgc, ctx, ctx_mod(1), gpost(1), gn, hgrn_w_out, (jm,))

        x = _ffn(x, mx(2), gpre(2), gpost(2), ffn_w_in, ffn_w_out, (i, 1))
        if not last:
            ctx = _ffn(ctx, ctx_mod(2), gpre(2), gpost(2), ffn_w_in, ffn_w_out, (i, 1))
    return x
```

```python
import functools

import jax
import jax.numpy as jnp
import numpy as np
from jax import lax
from jax.experimental import pallas as pl
from jax.experimental.pallas import tpu as pltpu

F32 = jnp.float32
BF16 = jnp.bfloat16

GRID_W = 64
FOURIER_GROUPS = 4
HGRN_HEADS = 8
N_MOD = 9
HALF = 0.5
NORM_EPS = 1e-6
LB_FLOOR = 1e-30

VMEM_LIMIT_V7X = 56 * 1024 * 1024
SUBLANES = 8
LANES = 128
MXU_COLS = 256
SCAN_TILE = 512
FAST_BLOCK = 16
FAST_RANGE = 100.0
SCAN_HEADS = 8
MOD_ROWS = 16


def _params(*sem):
    return pltpu.CompilerParams(dimension_semantics=sem, vmem_limit_bytes=VMEM_LIMIT_V7X)


def _const_spec(shape):
    n = len(shape)
    return pl.BlockSpec(shape, lambda *_: (0,) * n, pipeline_mode=pl.Buffered(1))


def _pick_spec(stacked, lead):
    tail = stacked.shape[len(lead):]
    return pl.BlockSpec((None,) * len(lead) + tail, lambda *_: tuple(lead) + (0,) * len(tail),
                        pipeline_mode=pl.Buffered(1))


def _rms(x, g):
    ms = jnp.mean(x * x, axis=-1, keepdims=True)
    return x * lax.rsqrt(ms + NORM_EPS) * g


def _pre(x, mod_ref, g_ref):
    return _rms(x, g_ref[...]) * (1.0 + mod_ref[1:2, :]) + mod_ref[0:1, :]


def _post(x, y, mod_ref, g_ref, w):
    return x + w * mod_ref[2:3, :] * _rms(y, g_ref[...])


def _silu(x):
    return x * jax.nn.sigmoid(x)


def _dot(a, b):
    return jnp.dot(a, b, preferred_element_type=F32)


def _dot_nt(a, b):
    return lax.dot_general(a, b, (((1,), (1,)), ((), ())), preferred_element_type=F32)


def _dot_tn(a, b):
    return lax.dot_general(a, b, (((0,), (0,)), ((), ())), preferred_element_type=F32)


def _adaln_kernel(c_ref, w_ref, b_ref, o_ref):
    sc = _silu(c_ref[...]).astype(BF16)
    o_ref[...] = _dot(sc, w_ref[...].astype(BF16)) + b_ref[...]


def _adaln(c_rows, ada_w, ada_b):
    depth, d, n = ada_w.shape
    tn = 1024
    return pl.pallas_call(
        _adaln_kernel,
        grid=(depth, n // tn),
        in_specs=[
            pl.BlockSpec((MOD_ROWS, d), lambda i, j: (0, 0)),
            pl.BlockSpec((None, d, tn), lambda i, j: (i, 0, j)),
            pl.BlockSpec((None, 1, tn), lambda i, j: (i, 0, j)),
        ],
        out_specs=pl.BlockSpec((None, MOD_ROWS, tn), lambda i, j: (i, 0, j)),
        out_shape=jax.ShapeDtypeStruct((depth, MOD_ROWS, n), F32),
        compiler_params=_params("parallel", "parallel"),
        name="adaln",
    )(c_rows, ada_w, ada_b.reshape(depth, 1, n))


def _ffn_kernel(x_ref, mod_ref, gpre_ref, gpost_ref, win_ref, wout_ref, o_ref, *, d_ff):
    x = x_ref[...]
    hb = _pre(x, mod_ref, gpre_ref).astype(BF16)
    fc = MXU_COLS
    acc = None
    for ci in range(d_ff // fc):
        gate = _dot(hb, win_ref[:, ci * fc:(ci + 1) * fc])
        up = _dot(hb, win_ref[:, d_ff + ci * fc:d_ff + (ci + 1) * fc])
        a = (_silu(gate) * up).astype(BF16)
        y = _dot(a, wout_ref[ci * fc:(ci + 1) * fc, :])
        acc = y if acc is None else acc + y
    o_ref[...] = _post(x, acc, mod_ref, gpost_ref, HALF)


def _token_tile(n_tokens, want):
    return want if n_tokens % want == 0 else n_tokens


def _ffn(s, mod, gpre, gpost, w_in, w_out, lead):
    b, l, d = s.shape
    d_ff = w_out.shape[-2]
    assert d_ff % MXU_COLS == 0
    tm = _token_tile(l, 1024)
    return pl.pallas_call(
        functools.partial(_ffn_kernel, d_ff=d_ff),
        grid=(b, l // tm),
        in_specs=[
            pl.BlockSpec((None, tm, d), lambda i, j: (i, j, 0)),
            pl.BlockSpec((None, 3, d), lambda i, j: (i, 0, 0)),
            _const_spec((1, d)),
            _const_spec((1, d)),
            _pick_spec(w_in, lead),
            _pick_spec(w_out, lead),
        ],
        out_specs=pl.BlockSpec((None, tm, d), lambda i, j: (i, j, 0)),
        out_shape=jax.ShapeDtypeStruct(s.shape, F32),
        compiler_params=_params("parallel", "parallel"),
        name="ffn",
    )(s, mod, gpre, gpost, w_in, w_out)


def _dft_cos_sin(n):
    k = np.arange(n)
    ang = 2.0 * np.pi * ((k[:, None] * k[None, :]) % n) / n
    s = 1.0 / np.sqrt(n)
    return np.cos(ang) * s, np.sin(ang) * s


def _fourier_consts(gd):
    c, s = _dft_cos_sin(gd)
    chan = np.concatenate([c, s], axis=1)
    c, s = _dft_cos_sin(GRID_W)
    col = np.block([[c, -s], [s, c]])
    return jnp.asarray(chan, BF16), jnp.asarray(col, BF16)


def _channel_dft(hb, chan_ref, gd):
    us, vs = [], []
    for g in range(FOURIER_GROUPS):
        uv = _dot(hb[:, g * gd:(g + 1) * gd], chan_ref[...])
        us.append(uv[:, :gd])
        vs.append(uv[:, gd:])
    return (jnp.concatenate(us, axis=1).astype(BF16), jnp.concatenate(vs, axis=1).astype(BF16))


def _pack_pair(a, b):
    hi = lax.bitcast_convert_type(a.astype(BF16).astype(F32), jnp.uint32)
    lo = lax.bitcast_convert_type(b.astype(BF16).astype(F32), jnp.uint32)
    return hi | (lo >> 16)


def _unpack_pair(p):
    a = lax.bitcast_convert_type(p & jnp.uint32(0xFFFF0000), F32)
    b = lax.bitcast_convert_type(p << 16, F32)
    return a.astype(BF16), b.astype(BF16)


def _fourier_cols_kernel(x_ref, mod_ref, gpre_ref, chan_ref, col_ref, ab_ref, *, rows_per_step, gd):
    hb = _pre(x_ref[...], mod_ref, gpre_ref).astype(BF16)
    u, v = _channel_dft(hb, chan_ref, gd)
    for r in range(rows_per_step):
        sl = slice(r * GRID_W, (r + 1) * GRID_W)
        ab = _dot(col_ref[...], jnp.concatenate([u[sl], v[sl]], axis=0))
        ab_ref[sl, :] = _pack_pair(ab[:GRID_W], ab[GRID_W:])


def _fourier_rows_kernel(ab_ref, x_ref, mod_ref, gpost_ref, rowk_ref, w_ref, o_ref):
    rows, cps, d = x_ref.shape
    flat = lambda v: v.reshape(rows * cps, d)
    a, b = _unpack_pair(flat(ab_ref[...]))
    y = _dot(rowk_ref[...], jnp.concatenate([a, b], axis=0)).astype(BF16)
    res = _post(flat(x_ref[...]), _dot(y, w_ref[...]), mod_ref, gpost_ref, 1.0)
    o_ref[...] = res.reshape(rows, cps, d)


def _fourier_latent(x, mod, gpre, gpost, w_out, lead):
    b, l, d = x.shape
    rows = l // GRID_W
    gd = d // FOURIER_GROUPS
    chan, col = _fourier_consts(gd)
    rps = 8 if rows % 8 == 0 else 1
    tm = rps * GRID_W
    ab = pl.pallas_call(
        functools.partial(_fourier_cols_kernel, rows_per_step=rps, gd=gd),
        grid=(b, l // tm),
        in_specs=[
            pl.BlockSpec((None, tm, d), lambda i, j: (i, j, 0)),
            pl.BlockSpec((None, 3, d), lambda i, j: (i, 0, 0)),
            _const_spec((1, d)),
            _const_spec(chan.shape),
            _const_spec(col.shape),
        ],
        out_specs=pl.BlockSpec((None, tm, d), lambda i, j: (i, j, 0)),
        out_shape=jax.ShapeDtypeStruct((b, l, d), jnp.uint32),
        compiler_params=_params("parallel", "parallel"),
        name="fourier_cols",
    )(x, mod, gpre, chan, col)

    c, s = _dft_cos_sin(rows)
    cps = SUBLANES
    eye = np.eye(cps)
    rowk = jnp.asarray(np.concatenate([np.kron(c, eye), -np.kron(s, eye)], axis=1), BF16)
    grid_view = lambda t: t.reshape(b, rows, GRID_W, d)
    blk = pl.BlockSpec((None, rows, cps, d), lambda i, j: (i, 0, j, 0))
    out = pl.pallas_call(
        _fourier_rows_kernel,
        grid=(b, GRID_W // cps),
        in_specs=[
            blk, blk,
            pl.BlockSpec((None, 3, d), lambda i, j: (i, 0, 0)),
            _const_spec((1, d)),
            _const_spec(rowk.shape),
            _pick_spec(w_out, lead),
        ],
        out_specs=blk,
        out_shape=jax.ShapeDtypeStruct((b, rows, GRID_W, d), F32),
        compiler_params=_params("parallel", "parallel"),
        name="fourier_rows",
    )(grid_view(ab), grid_view(x), mod, gpost, rowk, w_out)
    return out.reshape(b, l, d)


def _fourier_ctx_kernel(x_ref, mod_ref, gpre_ref, gpost_ref, chan_ref, seq_ref, w_ref, o_ref, *, gd):
    x = x_ref[...]
    hb = _pre(x, mod_ref, gpre_ref).astype(BF16)
    u, v = _channel_dft(hb, chan_ref, gd)
    y = _dot(seq_ref[...], jnp.concatenate([u, v], axis=0)).astype(BF16)
    o_ref[...] = _post(x, _dot(y, w_ref[...]), mod_ref, gpost_ref, 1.0)


def _fourier_ctx(x, mod, gpre, gpost, w_out, lead):
    b, l, d = x.shape
    gd = d // FOURIER_GROUPS
    chan, _ = _fourier_consts(gd)
    c, s = _dft_cos_sin(l)
    seqm = jnp.asarray(np.concatenate([c, -s], axis=1), BF16)
    return pl.pallas_call(
        functools.partial(_fourier_ctx_kernel, gd=gd),
        grid=(b,),
        in_specs=[
            pl.BlockSpec((None, l, d), lambda i: (i, 0, 0)),
            pl.BlockSpec((None, 3, d), lambda i: (0, 0, 0)),
            _const_spec((1, d)),
            _const_spec((1, d)),
            _const_spec(chan.shape),
            _const_spec(seqm.shape),
            _pick_spec(w_out, lead),
        ],
        out_specs=pl.BlockSpec((None, l, d), lambda i: (i, 0, 0)),
        out_shape=jax.ShapeDtypeStruct(x.shape, F32),
        compiler_params=_params("parallel"),
        name="fourier_ctx",
    )(x, mod, gpre, gpost, chan, seqm, w_out)


def _lower_bound(logits_ref, j):
    lg = logits_ref[...]
    e = jnp.exp(lg - jnp.max(lg, axis=0, keepdims=True))
    p = e / jnp.sum(e, axis=0, keepdims=True)
    lb = jnp.zeros_like(p[0:1])
    for i in range(1, j + 1):
        lb = lb + p[i:i + 1]
    return lb


def _forget_gate(z, lb):
    e = jnp.exp(-jnp.abs(z))
    inv = 1.0 / (1.0 + e)
    pos = z >= 0.0
    sig = jnp.where(pos, inv, e * inv)
    nsig = jnp.where(pos, e * inv, inv)
    f = jnp.maximum(lb, LB_FLOOR) + (1.0 - lb) * sig
    return (1.0 - lb) * nsig, jnp.log2(f)


def _split2(x):
    hi = x.astype(BF16)
    lo = (x - hi.astype(F32)).astype(BF16)
    return hi, lo


def _tile_cumsum(g, tri_ref, tile, reverse):
    c = tri_ref.shape[0]
    parts = _split2(g)
    chunks = [sum(_dot(tri_ref[...], p[i:i + c]) for p in parts) for i in range(0, g.shape[0], c)]
    per_tile = tile // c
    order = range(per_tile - 1, -1, -1) if reverse else range(per_tile)
    last = 0 if reverse else c - 1
    for t0 in range(0, len(chunks), per_tile):
        carry = None
        for i in order:
            if carry is not None:
                chunks[t0 + i] = chunks[t0 + i] + carry
            carry = chunks[t0 + i][last:last + 1, :]
    return jnp.concatenate(chunks, axis=0)


def _hgrn_in_kernel(x_ref, mod_ref, gpre_ref, w_ref, lbf_ref, lbb_ref, trif_ref, trib_ref,
                    q_ref, v_ref, kf_ref, bf_ref, kb_ref, bb_ref, sg_ref, *, layer_j, kd, hd, tile):
    hb = _pre(x_ref[...], mod_ref, gpre_ref).astype(BF16)
    col = lambda n: _dot(hb, w_ref[:, n * kd:(n + 1) * kd])
    q = _silu(col(0))
    v = col(1)
    kf, lf = _forget_gate(col(2), _lower_bound(lbf_ref, layer_j))
    kb, lbw = _forget_gate(col(3), _lower_bound(lbb_ref, layer_j))
    sg_ref[...] = _silu(col(4)).astype(BF16)
    bf = _tile_cumsum(lf, trif_ref, tile, False)
    bb = _tile_cumsum(lbw, trib_ref, tile, True)
    for h in range(HGRN_HEADS):
        sl = slice(h * hd, (h + 1) * hd)
        q_ref[h] = q[:, sl].astype(BF16)
        v_ref[h] = v[:, sl].astype(BF16)
        kf_ref[h] = kf[:, sl].astype(BF16)
        kb_ref[h] = kb[:, sl].astype(BF16)
        bf_ref[h] = bf[:, sl]
        bb_ref[h] = bb[:, sl]


def _hgrn_inputs(x, mod, gpre, w_in, lead, lb_fwd, lb_bwd, layer_j):
    b, l, d = x.shape
    kd = d
    hd = kd // HGRN_HEADS
    tm = _token_tile(l, 512)
    tile = _token_tile(tm, SCAN_TILE)
    t = _token_tile(tile, MXU_COLS)
    trif = _scan_consts(t, False)[0]
    trib = _scan_consts(t, True)[0]
    x_spec = pl.BlockSpec((None, tm, d), lambda i, j: (i, j, 0))
    head_spec = pl.BlockSpec((None, HGRN_HEADS, tm, hd), lambda i, j: (i, 0, j, 0))
    heads = lambda dt: jax.ShapeDtypeStruct((b, HGRN_HEADS, l, hd), dt)
    return pl.pallas_call(
        functools.partial(_hgrn_in_kernel, layer_j=layer_j, kd=kd, hd=hd, tile=tile),
        grid=(b, l // tm),
        in_specs=[
            x_spec,
            pl.BlockSpec((None, 3, d), lambda i, j: (i, 0, 0)),
            _const_spec((1, d)),
            _pick_spec(w_in, lead),
            _const_spec(lb_fwd.shape),
            _const_spec(lb_bwd.shape),
            _const_spec((t, t)),
            _const_spec((t, t)),
        ],
        out_specs=[head_spec] * 6 + [x_spec],
        out_shape=[heads(BF16), heads(BF16), heads(BF16), heads(F32), heads(BF16), heads(F32),
                   jax.ShapeDtypeStruct((b, l, d), BF16)],
        compiler_params=_params("parallel", "parallel"),
        name="hgrn_in",
    )(x, mod, gpre, w_in, lb_fwd, lb_bwd, trif, trib)


def _scan_consts(n, reverse):
    r = np.arange(n)[:, None]
    c = np.arange(n)[None, :]
    valid = (c >= r) if reverse else (c <= r)
    x = r ^ c
    lvl = np.where(x == 0, 0, np.floor(np.log2(np.maximum(x, 1))).astype(np.int64) + 1)
    fine = np.where(valid & (x < SUBLANES), lvl, -1)
    near = valid & (x < FAST_BLOCK)
    return jnp.asarray(valid, BF16), jnp.asarray(fine, jnp.int32), jnp.asarray(near, jnp.int32)


def _level_ref(b, m, reverse):
    t, n = b.shape
    first = m if reverse else m - 1
    b3 = b.reshape(t // SUBLANES, SUBLANES, n)
    rid = lax.broadcasted_iota(jnp.int32, b3.shape, 1)
    ref = None
    for p in range(0, SUBLANES, 2 * m):
        row = jnp.broadcast_to(b3[:, p + first:p + first + 1, :], b3.shape)
        ref = row if ref is None else jnp.where(rid >= p, row, ref)
    return ref.reshape(t, n)


def _neg_abs(d):
    bits = lax.bitcast_convert_type(d, jnp.int32) | jnp.int32(-2 ** 31)
    return lax.bitcast_convert_type(bits, F32)


def _scaled(z, e):
    return (z * e).astype(BF16)


def _halves(lo, m, reverse):
    return (lo, lo + m, lo + m) if reverse else (lo + m, lo, lo + m - 1)


def _strip_scores(q, k, q32, k32, b, fine, near, lane, reverse, fast):
    n = q.shape[0]
    if fast:
        first = FAST_BLOCK - 1 if reverse else 0
        ref = jnp.concatenate([jnp.broadcast_to(b[i + first:i + first + 1, :], (FAST_BLOCK, b.shape[1]))
                               for i in range(0, n, FAST_BLOCK)], axis=0)
        d = b - ref
        a = jnp.where(near == 1, _dot_nt(_scaled(q32, jnp.exp2(d)), _scaled(k32, jnp.exp2(-d))), 0.0)
        m = FAST_BLOCK
    else:
        a = jnp.where(fine == 0, _dot_nt(q, k), 0.0)
        before = pltpu.roll(b, (n - 1) if reverse else 1, 0)
        a = jnp.where(fine == 1, _dot_nt(_scaled(q32, jnp.exp2(_neg_abs(b - before))), k), a)
        m, level = 2, 2
        while m < SUBLANES:
            e = jnp.exp2(_neg_abs(b - _level_ref(b, m, reverse)))
            a = jnp.where(fine == level, _dot_nt(_scaled(q32, e), _scaled(k32, e)), a)
            m, level = 2 * m, level + 1
    blocks = [a[i:i + SUBLANES] for i in range(0, n, SUBLANES)]
    while m < n:
        ql, kf = [], []
        for lo in range(0, n, 2 * m):
            late, early, first = _halves(lo, m, reverse)
            ref = b[first:first + 1, :]
            ql.append(_scaled(q32[late:late + m], jnp.exp2(b[late:late + m] - ref)))
            ke = _scaled(k32[early:early + m], jnp.exp2(ref - b[early:early + m]))
            kf += [k[late:late + m], ke] if reverse else [ke, k[late:late + m]]
        s_m = _dot_nt(jnp.concatenate(ql, axis=0), jnp.concatenate(kf, axis=0))
        for pi, lo in enumerate(range(0, n, 2 * m)):
            late, early, _ = _halves(lo, m, reverse)
            inside = (lane >= early) & (lane < early + m)
            for i in range(0, m, SUBLANES):
                rb = (late + i) // SUBLANES
                blocks[rb] = jnp.where(inside, s_m[pi * m + i:pi * m + i + SUBLANES], blocks[rb])
        m *= 2
    return jnp.concatenate(blocks, axis=0)


def _scan_tile(q, k, v, b, st, fine, near, reverse, fast):
    t = q.shape[0]
    n = fine.shape[0]
    q32 = q.astype(F32)
    k32 = k.astype(F32)
    end = 0 if reverse else t - 1
    b_end = b[end:end + 1, :]

    o = _dot_nt(_scaled(q32, jnp.exp2(b)), st.astype(BF16))
    new_st = st * jnp.exp2(b_end) + _dot_tn(v, _scaled(k32, jnp.exp2(b_end - b)))

    lane = lax.broadcasted_iota(jnp.int32, (SUBLANES, n), 1)
    strips = range(0, t, n)
    score = {(i, i): _strip_scores(q[i:i + n], k[i:i + n], q32[i:i + n], k32[i:i + n], b[i:i + n],
                                   fine, near, lane, reverse, fast) for i in strips}
    m = n
    while m < t:
        for lo in range(0, t, 2 * m):
            late, early, first = _halves(lo, m, reverse)
            ref = b[first:first + 1, :]
            ql = _scaled(q32[late:late + m], jnp.exp2(b[late:late + m] - ref))
            ke = _scaled(k32[early:early + m], jnp.exp2(ref - b[early:early + m]))
            s_m = _dot_nt(ql, ke)
            for i in range(0, m, n):
                for j in range(0, m, n):
                    score[(late + i, early + j)] = s_m[i:i + n, j:j + n]
        m *= 2
    outs = []
    for i in strips:
        keys = [j for j in strips if (i, j) in score]
        a = jnp.concatenate([score[(i, j)] for j in keys], axis=1).astype(BF16)
        outs.append(_dot(a, jnp.concatenate([v[j:j + n] for j in keys], axis=0)))
    return o + jnp.concatenate(outs, axis=0), new_st


def _scan_kernel(q_ref, k_ref, v_ref, b_ref, s0_ref, fine_ref, near_ref, o_ref, sfin_ref, state_ref,
                 *, reverse, n_tiles):
    step = pl.program_id(2)

    @pl.when(step == 0)
    def _():
        state_ref[...] = s0_ref[...]

    hp, t, _ = q_ref.shape
    nblk = t // FAST_BLOCK
    spread = None
    for h in range(hp):
        top = b_ref[h, pl.ds(0, nblk, stride=FAST_BLOCK), :]
        bottom = b_ref[h, pl.ds(FAST_BLOCK - 1, nblk, stride=FAST_BLOCK), :]
        s = jnp.max(jnp.abs(top - bottom))
        spread = s if spread is None else jnp.maximum(spread, s)
    fast_ok = spread <= FAST_RANGE

    def run(fast):
        fine = fine_ref[...]
        near = near_ref[...]
        for h in range(hp):
            o, st = _scan_tile(q_ref[h], k_ref[h], v_ref[h], b_ref[h], state_ref[h], fine, near, reverse, fast)
            state_ref[h] = st
            o_ref[h] = o.astype(o_ref.dtype)

    pl.when(fast_ok)(functools.partial(run, True))
    pl.when(jnp.logical_not(fast_ok))(functools.partial(run, False))

    @pl.when(step == n_tiles - 1)
    def _():
        sfin_ref[...] = state_ref[...]


def _scan(q, k, v, bcum, s0, reverse):
    b, h, l, hd = q.shape
    t = _token_tile(l, SCAN_TILE)
    n_tiles = l // t
    hp = SCAN_HEADS
    _, fine, near = _scan_consts(LANES, reverse)
    tile = (lambda j: n_tiles - 1 - j) if reverse else (lambda j: j)
    tok_spec = pl.BlockSpec((None, hp, t, hd), lambda i, hh, j: (i, hh, tile(j), 0))
    st_spec = pl.BlockSpec((None, hp, hd, hd), lambda i, hh, j: (i, hh, 0, 0))
    return pl.pallas_call(
        functools.partial(_scan_kernel, reverse=reverse, n_tiles=n_tiles),
        grid=(b, h // hp, n_tiles),
        in_specs=[tok_spec, tok_spec, tok_spec, tok_spec, st_spec, _const_spec(fine.shape), _const_spec(near.shape)],
        out_specs=[tok_spec, st_spec],
        out_shape=[jax.ShapeDtypeStruct((b, h, l, hd), BF16), jax.ShapeDtypeStruct((b, h, hd, hd), F32)],
        scratch_shapes=[pltpu.VMEM((hp, hd, hd), F32)],
        compiler_params=_params("parallel", "parallel", "arbitrary"),
        name="scan_bwd" if reverse else "scan_fwd",
    )(q, k, v, bcum, s0, fine, near)


def _hgrn_out_kernel(of_ref, ob_ref, sg_ref, x_ref, mod_ref, gpost_ref, gn_ref, w_ref, o_ref):
    heads = []
    for h in range(HGRN_HEADS):
        heads.append(_rms(of_ref[h].astype(F32) + ob_ref[h].astype(F32), gn_ref[...]))
    o = jnp.concatenate(heads, axis=1)
    y = _dot((o * sg_ref[...].astype(F32)).astype(BF16), w_ref[...])
    o_ref[...] = _post(x_ref[...], y, mod_ref, gpost_ref, 1.0)


def _hgrn_readout(o_f, o_b, sg, x, mod, gpost, g_norm, w_out, lead):
    b, l, d = x.shape
    hd = d // HGRN_HEADS
    tm = _token_tile(l, 512)
    x_spec = pl.BlockSpec((None, tm, d), lambda i, j: (i, j, 0))
    head_spec = pl.BlockSpec((None, HGRN_HEADS, tm, hd), lambda i, j: (i, 0, j, 0))
    return pl.pallas_call(
        _hgrn_out_kernel,
        grid=(b, l // tm),
        in_specs=[
            head_spec, head_spec, x_spec, x_spec,
            pl.BlockSpec((None, 3, d), lambda i, j: (i, 0, 0)),
            _const_spec((1, d)),
            _const_spec((1, hd)),
            _pick_spec(w_out, lead),
        ],
        out_specs=x_spec,
        out_shape=jax.ShapeDtypeStruct(x.shape, F32),
        compiler_params=_params("parallel", "parallel"),
        name="hgrn_out",
    )(o_f, o_b, sg, x, mod, gpost, g_norm, w_out)


def kernel(x, c, ctx, c_ctx, ada_w, ada_b, norm_pre, norm_post, ffn_w_in, ffn_w_out, fourier_w_out,
           hgrn_w_in, hgrn_lb_fwd, hgrn_lb_bwd, hgrn_norm, hgrn_w_out):
    bsz, _, d = x.shape
    depth = ada_w.shape[0]
    assert bsz + 1 <= MOD_ROWS

    c_rows = jnp.concatenate([c, c_ctx[None, :], jnp.zeros((MOD_ROWS - bsz - 1, d), F32)], axis=0)
    mod = _adaln(c_rows, ada_w, ada_b).reshape(depth, MOD_ROWS, N_MOD, d)

    ffn_w_in = ffn_w_in.astype(BF16)
    ffn_w_out = ffn_w_out.astype(BF16)
    fourier_w_out = fourier_w_out.astype(BF16)
    hgrn_w_in = hgrn_w_in.astype(BF16)
    hgrn_w_out = hgrn_w_out.astype(BF16)

    for i in range(depth):
        last = i == depth - 1
        is_hgrn = i % 2 == 1
        jm = i // 2
        mx = lambda j: mod[i, :bsz, 3 * j:3 * j + 3]
        mc = lambda j: mod[i, bsz:bsz + 1, 3 * j:3 * j + 3]
        gpre = lambda j: norm_pre[i, j][None, :]
        gpost = lambda j: norm_post[i, j][None, :]
        ctx_mod = lambda j: jnp.broadcast_to(mc(j), (bsz, 3, d))

        x = _ffn(x, mx(0), gpre(0), gpost(0), ffn_w_in, ffn_w_out, (i, 0))
        if is_hgrn or not last:
            ctx = _ffn(ctx, ctx_mod(0), gpre(0), gpost(0), ffn_w_in, ffn_w_out, (i, 0))

        if not is_hgrn:
            x = _fourier_latent(x, mx(1), gpre(1), gpost(1), fourier_w_out, (jm,))
            if not last:
                ctx = _fourier_ctx(ctx, mc(1), gpre(1), gpost(1), fourier_w_out, (jm,))
        else:
            hd = d // HGRN_HEADS
            zero = jnp.zeros((bsz, HGRN_HEADS, hd, hd), F32)
            qc, vc, kfc, bfc, kbc, bbc, sgc = _hgrn_inputs(
                ctx, ctx_mod(1), gpre(1), hgrn_w_in, (jm,), hgrn_lb_fwd, hgrn_lb_bwd, jm)
            ocf, s_f = _scan(qc, kfc, vc, bfc, zero, False)
            ocb, s_b = _scan(qc, kbc, vc, bbc, zero, True)
            qx, vx, kfx, bfx, kbx, bbx, sgx = _hgrn_inputs(
                x, mx(1), gpre(1), hgrn_w_in, (jm,), hgrn_lb_fwd, hgrn_lb_bwd, jm)
            oxf, _ = _scan(qx, kfx, vx, bfx, s_f, False)
            oxb, _ = _scan(qx, kbx, vx, bbx, s_b, True)
            gn = hgrn_norm[jm][None, :]
            x = _hgrn_readout(oxf, oxb, sgx, x, mx(1), gpost(1), gn, hgrn_w_out, (jm,))
            if not last:
                ctx = _hgrn_readout(ocf, ocb, sgc, ctx, ctx_mod(1), gpost(1), gn, hgrn_w_out, (jm,))

        x = _ffn(x, mx(2), gpre(2), gpost(2), ffn_w_in, ffn_w_out, (i, 1))
        if not last:
            ctx = _ffn(ctx, ctx_mod(2), gpre(2), gpost(2), ffn_w_in, ffn_w_out, (i, 1))
    return x
```

```python
import functools

import jax
import jax.numpy as jnp
import numpy as np
from jax import lax
from jax.experimental import pallas as pl
from jax.experimental.pallas import tpu as pltpu

F32 = jnp.float32
BF16 = jnp.bfloat16

GRID_W = 64
FOURIER_GROUPS = 4
HGRN_HEADS = 8
N_MOD = 9
HALF = 0.5
NORM_EPS = 1e-6
LB_FLOOR = 1e-30

VMEM_LIMIT_V7X = 56 * 1024 * 1024
SUBLANES = 8
LANES = 128
MXU_COLS = 256
SCAN_TILE = 512
FAST_BLOCKS = (64, 32, 16)
FAST_RANGE = 110.0
SCAN_HEADS = 8
MOD_ROWS = 16


def _params(*sem):
    return pltpu.CompilerParams(dimension_semantics=sem, vmem_limit_bytes=VMEM_LIMIT_V7X)


def _const_spec(shape):
    n = len(shape)
    return pl.BlockSpec(shape, lambda *_: (0,) * n, pipeline_mode=pl.Buffered(1))


def _pick_spec(stacked, lead):
    tail = stacked.shape[len(lead):]
    return pl.BlockSpec((None,) * len(lead) + tail, lambda *_: tuple(lead) + (0,) * len(tail),
                        pipeline_mode=pl.Buffered(1))


def _rms(x, g):
    ms = jnp.mean(x * x, axis=-1, keepdims=True)
    return x * lax.rsqrt(ms + NORM_EPS) * g


def _pre(x, mod_ref, g_ref):
    return _rms(x, g_ref[...]) * (1.0 + mod_ref[1:2, :]) + mod_ref[0:1, :]


def _post(x, y, mod_ref, g_ref, w):
    return x + w * mod_ref[2:3, :] * _rms(y, g_ref[...])


def _silu(x):
    return x * jax.nn.sigmoid(x)


def _dot(a, b):
    return jnp.dot(a, b, preferred_element_type=F32)


def _dot_nt(a, b):
    return lax.dot_general(a, b, (((1,), (1,)), ((), ())), preferred_element_type=F32)


def _dot_tn(a, b):
    return lax.dot_general(a, b, (((0,), (0,)), ((), ())), preferred_element_type=F32)


def _adaln_kernel(c_ref, w_ref, b_ref, o_ref):
    sc = _silu(c_ref[...]).astype(BF16)
    o_ref[...] = _dot(sc, w_ref[...].astype(BF16)) + b_ref[...]


def _adaln(c_rows, ada_w, ada_b):
    depth, d, n = ada_w.shape
    tn = 1024
    return pl.pallas_call(
        _adaln_kernel,
        grid=(depth, n // tn),
        in_specs=[
            pl.BlockSpec((MOD_ROWS, d), lambda i, j: (0, 0)),
            pl.BlockSpec((None, d, tn), lambda i, j: (i, 0, j)),
            pl.BlockSpec((None, 1, tn), lambda i, j: (i, 0, j)),
        ],
        out_specs=pl.BlockSpec((None, MOD_ROWS, tn), lambda i, j: (i, 0, j)),
        out_shape=jax.ShapeDtypeStruct((depth, MOD_ROWS, n), F32),
        compiler_params=_params("parallel", "parallel"),
        name="adaln",
    )(c_rows, ada_w, ada_b.reshape(depth, 1, n))


def _ffn_kernel(x_ref, mod_ref, gpre_ref, gpost_ref, win_ref, wout_ref, o_ref, *, d_ff):
    x = x_ref[...]
    hb = _pre(x, mod_ref, gpre_ref).astype(BF16)
    fc = MXU_COLS
    acc = None
    for ci in range(d_ff // fc):
        gate = _dot(hb, win_ref[:, ci * fc:(ci + 1) * fc])
        up = _dot(hb, win_ref[:, d_ff + ci * fc:d_ff + (ci + 1) * fc])
        a = (_silu(gate) * up).astype(BF16)
        y = _dot(a, wout_ref[ci * fc:(ci + 1) * fc, :])
        acc = y if acc is None else acc + y
    o_ref[...] = _post(x, acc, mod_ref, gpost_ref, HALF)


def _token_tile(n_tokens, want):
    return want if n_tokens % want == 0 else n_tokens


def _ffn(s, mod, gpre, gpost, w_in, w_out, lead):
    b, l, d = s.shape
    d_ff = w_out.shape[-2]
    assert d_ff % MXU_COLS == 0
    tm = _token_tile(l, 1024)
    return pl.pallas_call(
        functools.partial(_ffn_kernel, d_ff=d_ff),
        grid=(b, l // tm),
        in_specs=[
            pl.BlockSpec((None, tm, d), lambda i, j: (i, j, 0)),
            pl.BlockSpec((None, 3, d), lambda i, j: (i, 0, 0)),
            _const_spec((1, d)),
            _const_spec((1, d)),
            _pick_spec(w_in, lead),
            _pick_spec(w_out, lead),
        ],
        out_specs=pl.BlockSpec((None, tm, d), lambda i, j: (i, j, 0)),
        out_shape=jax.ShapeDtypeStruct(s.shape, F32),
        compiler_params=_params("parallel", "parallel"),
        name="ffn",
    )(s, mod, gpre, gpost, w_in, w_out)


def _dft_cos_sin(n):
    k = np.arange(n)
    ang = 2.0 * np.pi * ((k[:, None] * k[None, :]) % n) / n
    s = 1.0 / np.sqrt(n)
    return np.cos(ang) * s, np.sin(ang) * s


def _fourier_consts(gd):
    c, s = _dft_cos_sin(gd)
    chan = np.concatenate([c, s], axis=1)
    c, s = _dft_cos_sin(GRID_W)
    col = np.block([[c, -s], [s, c]])
    return jnp.asarray(chan, BF16), jnp.asarray(col, BF16)


def _channel_dft(hb, chan_ref, gd):
    us, vs = [], []
    for g in range(FOURIER_GROUPS):
        uv = _dot(hb[:, g * gd:(g + 1) * gd], chan_ref[...])
        us.append(uv[:, :gd])
        vs.append(uv[:, gd:])
    return (jnp.concatenate(us, axis=1).astype(BF16), jnp.concatenate(vs, axis=1).astype(BF16))


def _pack_pair(a, b):
    hi = lax.bitcast_convert_type(a.astype(BF16).astype(F32), jnp.uint32)
    lo = lax.bitcast_convert_type(b.astype(BF16).astype(F32), jnp.uint32)
    return hi | (lo >> 16)


def _unpack_pair(p):
    a = lax.bitcast_convert_type(p & jnp.uint32(0xFFFF0000), F32)
    b = lax.bitcast_convert_type(p << 16, F32)
    return a.astype(BF16), b.astype(BF16)


def _fourier_cols_kernel(x_ref, mod_ref, gpre_ref, chan_ref, col_ref, ab_ref, *, rows_per_step, gd):
    hb = _pre(x_ref[...], mod_ref, gpre_ref).astype(BF16)
    u, v = _channel_dft(hb, chan_ref, gd)
    for r in range(rows_per_step):
        sl = slice(r * GRID_W, (r + 1) * GRID_W)
        ab = _dot(col_ref[...], jnp.concatenate([u[sl], v[sl]], axis=0))
        ab_ref[sl, :] = _pack_pair(ab[:GRID_W], ab[GRID_W:])


def _fourier_rows_kernel(ab_ref, x_ref, mod_ref, gpost_ref, rowk_ref, w_ref, o_ref):
    rows, cps, d = x_ref.shape
    flat = lambda v: v.reshape(rows * cps, d)
    a, b = _unpack_pair(flat(ab_ref[...]))
    y = _dot(rowk_ref[...], jnp.concatenate([a, b], axis=0)).astype(BF16)
    res = _post(flat(x_ref[...]), _dot(y, w_ref[...]), mod_ref, gpost_ref, 1.0)
    o_ref[...] = res.reshape(rows, cps, d)


def _fourier_latent(x, mod, gpre, gpost, w_out, lead):
    b, l, d = x.shape
    rows = l // GRID_W
    gd = d // FOURIER_GROUPS
    chan, col = _fourier_consts(gd)
    rps = 8 if rows % 8 == 0 else 1
    tm = rps * GRID_W
    ab = pl.pallas_call(
        functools.partial(_fourier_cols_kernel, rows_per_step=rps, gd=gd),
        grid=(b, l // tm),
        in_specs=[
            pl.BlockSpec((None, tm, d), lambda i, j: (i, j, 0)),
            pl.BlockSpec((None, 3, d), lambda i, j: (i, 0, 0)),
            _const_spec((1, d)),
            _const_spec(chan.shape),
            _const_spec(col.shape),
        ],
        out_specs=pl.BlockSpec((None, tm, d), lambda i, j: (i, j, 0)),
        out_shape=jax.ShapeDtypeStruct((b, l, d), jnp.uint32),
        compiler_params=_params("parallel", "parallel"),
        name="fourier_cols",
    )(x, mod, gpre, chan, col)

    c, s = _dft_cos_sin(rows)
    cps = SUBLANES
    eye = np.eye(cps)
    rowk = jnp.asarray(np.concatenate([np.kron(c, eye), -np.kron(s, eye)], axis=1), BF16)
    grid_view = lambda t: t.reshape(b, rows, GRID_W, d)
    blk = pl.BlockSpec((None, rows, cps, d), lambda i, j: (i, 0, j, 0))
    out = pl.pallas_call(
        _fourier_rows_kernel,
        grid=(b, GRID_W // cps),
        in_specs=[
            blk, blk,
            pl.BlockSpec((None, 3, d), lambda i, j: (i, 0, 0)),
            _const_spec((1, d)),
            _const_spec(rowk.shape),
            _pick_spec(w_out, lead),
        ],
        out_specs=blk,
        out_shape=jax.ShapeDtypeStruct((b, rows, GRID_W, d), F32),
        compiler_params=_params("parallel", "parallel"),
        name="fourier_rows",
    )(grid_view(ab), grid_view(x), mod, gpost, rowk, w_out)
    return out.reshape(b, l, d)


def _fourier_ctx_kernel(x_ref, mod_ref, gpre_ref, gpost_ref, chan_ref, seq_ref, w_ref, o_ref, *, gd):
    x = x_ref[...]
    hb = _pre(x, mod_ref, gpre_ref).astype(BF16)
    u, v = _channel_dft(hb, chan_ref, gd)
    y = _dot(seq_ref[...], jnp.concatenate([u, v], axis=0)).astype(BF16)
    o_ref[...] = _post(x, _dot(y, w_ref[...]), mod_ref, gpost_ref, 1.0)


def _fourier_ctx(x, mod, gpre, gpost, w_out, lead):
    b, l, d = x.shape
    gd = d // FOURIER_GROUPS
    chan, _ = _fourier_consts(gd)
    c, s = _dft_cos_sin(l)
    seqm = jnp.asarray(np.concatenate([c, -s], axis=1), BF16)
    return pl.pallas_call(
        functools.partial(_fourier_ctx_kernel, gd=gd),
        grid=(b,),
        in_specs=[
            pl.BlockSpec((None, l, d), lambda i: (i, 0, 0)),
            pl.BlockSpec((None, 3, d), lambda i: (0, 0, 0)),
            _const_spec((1, d)),
            _const_spec((1, d)),
            _const_spec(chan.shape),
            _const_spec(seqm.shape),
            _pick_spec(w_out, lead),
        ],
        out_specs=pl.BlockSpec((None, l, d), lambda i: (i, 0, 0)),
        out_shape=jax.ShapeDtypeStruct(x.shape, F32),
        compiler_params=_params("parallel"),
        name="fourier_ctx",
    )(x, mod, gpre, gpost, chan, seqm, w_out)


def _lower_bound(logits_ref, j):
    lg = logits_ref[...]
    e = jnp.exp(lg - jnp.max(lg, axis=0, keepdims=True))
    p = e / jnp.sum(e, axis=0, keepdims=True)
    lb = jnp.zeros_like(p[0:1])
    for i in range(1, j + 1):
        lb = lb + p[i:i + 1]
    return lb


def _forget_gate(z, lb):
    e = jnp.exp(-jnp.abs(z))
    inv = 1.0 / (1.0 + e)
    pos = z >= 0.0
    sig = jnp.where(pos, inv, e * inv)
    nsig = jnp.where(pos, e * inv, inv)
    f = jnp.maximum(lb, LB_FLOOR) + (1.0 - lb) * sig
    return (1.0 - lb) * nsig, jnp.log2(f)


def _split2(x):
    hi = x.astype(BF16)
    lo = (x - hi.astype(F32)).astype(BF16)
    return hi, lo


def _tile_cumsum(g, tri_ref, tile, reverse):
    c = tri_ref.shape[0]
    parts = _split2(g)
    chunks = [sum(_dot(tri_ref[...], p[i:i + c]) for p in parts) for i in range(0, g.shape[0], c)]
    per_tile = tile // c
    order = range(per_tile - 1, -1, -1) if reverse else range(per_tile)
    last = 0 if reverse else c - 1
    for t0 in range(0, len(chunks), per_tile):
        carry = None
        for i in order:
            if carry is not None:
                chunks[t0 + i] = chunks[t0 + i] + carry
            carry = chunks[t0 + i][last:last + 1, :]
    return jnp.concatenate(chunks, axis=0)


def _hgrn_in_kernel(x_ref, mod_ref, gpre_ref, w_ref, lbf_ref, lbb_ref, trif_ref, trib_ref,
                    q_ref, v_ref, kf_ref, bf_ref, kb_ref, bb_ref, sg_ref, *, layer_j, kd, hd, tile):
    hb = _pre(x_ref[...], mod_ref, gpre_ref).astype(BF16)
    col = lambda n: _dot(hb, w_ref[:, n * kd:(n + 1) * kd])
    q = _silu(col(0))
    v = col(1)
    kf, lf = _forget_gate(col(2), _lower_bound(lbf_ref, layer_j))
    kb, lbw = _forget_gate(col(3), _lower_bound(lbb_ref, layer_j))
    sg_ref[...] = _silu(col(4)).astype(BF16)
    bf = _tile_cumsum(lf, trif_ref, tile, False)
    bb = _tile_cumsum(lbw, trib_ref, tile, True)
    for h in range(HGRN_HEADS):
        sl = slice(h * hd, (h + 1) * hd)
        q_ref[h] = q[:, sl].astype(BF16)
        v_ref[h] = v[:, sl].astype(BF16)
        kf_ref[h] = kf[:, sl].astype(BF16)
        kb_ref[h] = kb[:, sl].astype(BF16)
        bf_ref[h] = bf[:, sl]
        bb_ref[h] = bb[:, sl]


def _hgrn_inputs(x, mod, gpre, w_in, lead, lb_fwd, lb_bwd, layer_j):
    b, l, d = x.shape
    kd = d
    hd = kd // HGRN_HEADS
    tm = _token_tile(l, 512)
    tile = _token_tile(tm, SCAN_TILE)
    t = _token_tile(tile, MXU_COLS)
    trif = _scan_consts(t, False)[0]
    trib = _scan_consts(t, True)[0]
    x_spec = pl.BlockSpec((None, tm, d), lambda i, j: (i, j, 0))
    head_spec = pl.BlockSpec((None, HGRN_HEADS, tm, hd), lambda i, j: (i, 0, j, 0))
    heads = lambda dt: jax.ShapeDtypeStruct((b, HGRN_HEADS, l, hd), dt)
    return pl.pallas_call(
        functools.partial(_hgrn_in_kernel, layer_j=layer_j, kd=kd, hd=hd, tile=tile),
        grid=(b, l // tm),
        in_specs=[
            x_spec,
            pl.BlockSpec((None, 3, d), lambda i, j: (i, 0, 0)),
            _const_spec((1, d)),
            _pick_spec(w_in, lead),
            _const_spec(lb_fwd.shape),
            _const_spec(lb_bwd.shape),
            _const_spec((t, t)),
            _const_spec((t, t)),
        ],
        out_specs=[head_spec] * 6 + [x_spec],
        out_shape=[heads(BF16), heads(BF16), heads(BF16), heads(F32), heads(BF16), heads(F32),
                   jax.ShapeDtypeStruct((b, l, d), BF16)],
        compiler_params=_params("parallel", "parallel"),
        name="hgrn_in",
    )(x, mod, gpre, w_in, lb_fwd, lb_bwd, trif, trib)


def _scan_consts(n, reverse):
    r = np.arange(n)[:, None]
    c = np.arange(n)[None, :]
    valid = (c >= r) if reverse else (c <= r)
    x = r ^ c
    lvl = np.where(x == 0, 0, np.floor(np.log2(np.maximum(x, 1))).astype(np.int64) + 1)
    fine = np.where(valid & (x < SUBLANES), lvl, -1)
    gap = np.where(valid, x, n)
    return jnp.asarray(valid, BF16), jnp.asarray(fine, jnp.int32), jnp.asarray(gap, jnp.int32)


def _level_ref(b, m, reverse):
    t, n = b.shape
    first = m if reverse else m - 1
    b3 = b.reshape(t // SUBLANES, SUBLANES, n)
    rid = lax.broadcasted_iota(jnp.int32, b3.shape, 1)
    ref = None
    for p in range(0, SUBLANES, 2 * m):
        row = jnp.broadcast_to(b3[:, p + first:p + first + 1, :], b3.shape)
        ref = row if ref is None else jnp.where(rid >= p, row, ref)
    return ref.reshape(t, n)


def _neg_abs(d):
    bits = lax.bitcast_convert_type(d, jnp.int32) | jnp.int32(-2 ** 31)
    return lax.bitcast_convert_type(bits, F32)


def _scaled(z, e):
    return (z * e).astype(BF16)


def _halves(lo, m, reverse):
    return (lo, lo + m, lo + m) if reverse else (lo + m, lo, lo + m - 1)


def _strip_scores(q, k, q32, k32, b, fine, gap, lane, reverse, fast):
    n = q.shape[0]
    if fast:
        first = fast - 1 if reverse else 0
        ref = jnp.concatenate([jnp.broadcast_to(b[i + first:i + first + 1, :], (fast, b.shape[1]))
                               for i in range(0, n, fast)], axis=0)
        d = b - ref
        a = jnp.where(gap < fast, _dot_nt(_scaled(q32, jnp.exp2(d)), _scaled(k32, jnp.exp2(-d))), 0.0)
        m = fast
    else:
        a = jnp.where(fine == 0, _dot_nt(q, k), 0.0)
        before = pltpu.roll(b, (n - 1) if reverse else 1, 0)
        a = jnp.where(fine == 1, _dot_nt(_scaled(q32, jnp.exp2(_neg_abs(b - before))), k), a)
        m, level = 2, 2
        while m < SUBLANES:
            e = jnp.exp2(_neg_abs(b - _level_ref(b, m, reverse)))
            a = jnp.where(fine == level, _dot_nt(_scaled(q32, e), _scaled(k32, e)), a)
            m, level = 2 * m, level + 1
    blocks = [a[i:i + SUBLANES] for i in range(0, n, SUBLANES)]
    while m < n:
        ql, kf = [], []
        for lo in range(0, n, 2 * m):
            late, early, first = _halves(lo, m, reverse)
            ref = b[first:first + 1, :]
            ql.append(_scaled(q32[late:late + m], jnp.exp2(b[late:late + m] - ref)))
            ke = _scaled(k32[early:early + m], jnp.exp2(ref - b[early:early + m]))
            kf += [k[late:late + m], ke] if reverse else [ke, k[late:late + m]]
        s_m = _dot_nt(jnp.concatenate(ql, axis=0), jnp.concatenate(kf, axis=0))
        for pi, lo in enumerate(range(0, n, 2 * m)):
            late, early, _ = _halves(lo, m, reverse)
            inside = (lane >= early) & (lane < early + m)
            for i in range(0, m, SUBLANES):
                rb = (late + i) // SUBLANES
                blocks[rb] = jnp.where(inside, s_m[pi * m + i:pi * m + i + SUBLANES], blocks[rb])
        m *= 2
    return jnp.concatenate(blocks, axis=0)


def _scan_tile(q, k, v, b, st, fine, gap, reverse, fast):
    t = q.shape[0]
    n = fine.shape[0]
    q32 = q.astype(F32)
    k32 = k.astype(F32)
    end = 0 if reverse else t - 1
    b_end = b[end:end + 1, :]

    o = _dot_nt(_scaled(q32, jnp.exp2(b)), st.astype(BF16))
    new_st = st * jnp.exp2(b_end) + _dot_tn(v, _scaled(k32, jnp.exp2(b_end - b)))

    lane = lax.broadcasted_iota(jnp.int32, (SUBLANES, n), 1)
    strips = range(0, t, n)
    score = {(i, i): _strip_scores(q[i:i + n], k[i:i + n], q32[i:i + n], k32[i:i + n], b[i:i + n],
                                   fine, gap, lane, reverse, fast) for i in strips}
    m = n
    while m < t:
        for lo in range(0, t, 2 * m):
            late, early, first = _halves(lo, m, reverse)
            ref = b[first:first + 1, :]
            ql = _scaled(q32[late:late + m], jnp.exp2(b[late:late + m] - ref))
            ke = _scaled(k32[early:early + m], jnp.exp2(ref - b[early:early + m]))
            s_m = _dot_nt(ql, ke)
            for i in range(0, m, n):
                for j in range(0, m, n):
                    score[(late + i, early + j)] = s_m[i:i + n, j:j + n]
        m *= 2
    outs = []
    for i in strips:
        keys = [j for j in strips if (i, j) in score]
        a = jnp.concatenate([score[(i, j)] for j in keys], axis=1).astype(BF16)
        outs.append(_dot(a, jnp.concatenate([v[j:j + n] for j in keys], axis=0)))
    return o + jnp.concatenate(outs, axis=0), new_st


def _scan_kernel(q_ref, k_ref, v_ref, b_ref, s0_ref, fine_ref, gap_ref, o_ref, sfin_ref, state_ref,
                 *, reverse, n_tiles):
    step = pl.program_id(2)

    @pl.when(step == 0)
    def _():
        state_ref[...] = s0_ref[...]

    hp, t, _ = q_ref.shape

    def spread(blk):
        worst = None
        for h in range(hp):
            top = b_ref[h, pl.ds(0, t // blk, stride=blk), :]
            bottom = b_ref[h, pl.ds(blk - 1, t // blk, stride=blk), :]
            s = jnp.abs(top - bottom)
            worst = s if worst is None else jnp.maximum(worst, s)
        return jnp.max(worst)

    def run(fast):
        fine = fine_ref[...]
        gap = gap_ref[...]
        for h in range(hp):
            o, st = _scan_tile(q_ref[h], k_ref[h], v_ref[h], b_ref[h], state_ref[h], fine, gap, reverse, fast)
            state_ref[h] = st
            o_ref[h] = o.astype(o_ref.dtype)

    blocks = [blk for blk in FAST_BLOCKS if t % blk == 0]
    fits = [spread(blk) <= FAST_RANGE for blk in blocks]
    unresolved = None
    for blk, ok in zip(blocks, fits):
        pl.when(ok if unresolved is None else unresolved & ok)(functools.partial(run, blk))
        unresolved = jnp.logical_not(ok) if unresolved is None else unresolved & jnp.logical_not(ok)
    if unresolved is None:
        run(0)
    else:
        pl.when(unresolved)(functools.partial(run, 0))

    @pl.when(step == n_tiles - 1)
    def _():
        sfin_ref[...] = state_ref[...]


def _scan(q, k, v, bcum, s0, reverse):
    b, h, l, hd = q.shape
    t = _token_tile(l, SCAN_TILE)
    n_tiles = l // t
    hp = SCAN_HEADS
    _, fine, gap = _scan_consts(LANES, reverse)
    tile = (lambda j: n_tiles - 1 - j) if reverse else (lambda j: j)
    tok_spec = pl.BlockSpec((None, hp, t, hd), lambda i, hh, j: (i, hh, tile(j), 0))
    st_spec = pl.BlockSpec((None, hp, hd, hd), lambda i, hh, j: (i, hh, 0, 0))
    return pl.pallas_call(
        functools.partial(_scan_kernel, reverse=reverse, n_tiles=n_tiles),
        grid=(b, h // hp, n_tiles),
        in_specs=[tok_spec, tok_spec, tok_spec, tok_spec, st_spec, _const_spec(fine.shape), _const_spec(gap.shape)],
        out_specs=[tok_spec, st_spec],
        out_shape=[jax.ShapeDtypeStruct((b, h, l, hd), BF16), jax.ShapeDtypeStruct((b, h, hd, hd), F32)],
        scratch_shapes=[pltpu.VMEM((hp, hd, hd), F32)],
        compiler_params=_params("parallel", "parallel", "arbitrary"),
        name="scan_bwd" if reverse else "scan_fwd",
    )(q, k, v, bcum, s0, fine, gap)


def _hgrn_out_kernel(of_ref, ob_ref, sg_ref, x_ref, mod_ref, gpost_ref, gn_ref, w_ref, o_ref):
    heads = []
    for h in range(HGRN_HEADS):
        heads.append(_rms(of_ref[h].astype(F32) + ob_ref[h].astype(F32), gn_ref[...]))
    o = jnp.concatenate(heads, axis=1)
    y = _dot((o * sg_ref[...].astype(F32)).astype(BF16), w_ref[...])
    o_ref[...] = _post(x_ref[...], y, mod_ref, gpost_ref, 1.0)


def _hgrn_readout(o_f, o_b, sg, x, mod, gpost, g_norm, w_out, lead):
    b, l, d = x.shape
    hd = d // HGRN_HEADS
    tm = _token_tile(l, 512)
    x_spec = pl.BlockSpec((None, tm, d), lambda i, j: (i, j, 0))
    head_spec = pl.BlockSpec((None, HGRN_HEADS, tm, hd), lambda i, j: (i, 0, j, 0))
    return pl.pallas_call(
        _hgrn_out_kernel,
        grid=(b, l // tm),
        in_specs=[
            head_spec, head_spec, x_spec, x_spec,
            pl.BlockSpec((None, 3, d), lambda i, j: (i, 0, 0)),
            _const_spec((1, d)),
            _const_spec((1, hd)),
            _pick_spec(w_out, lead),
        ],
        out_specs=x_spec,
        out_shape=jax.ShapeDtypeStruct(x.shape, F32),
        compiler_params=_params("parallel", "parallel"),
        name="hgrn_out",
    )(o_f, o_b, sg, x, mod, gpost, g_norm, w_out)


def kernel(x, c, ctx, c_ctx, ada_w, ada_b, norm_pre, norm_post, ffn_w_in, ffn_w_out, fourier_w_out,
           hgrn_w_in, hgrn_lb_fwd, hgrn_lb_bwd, hgrn_norm, hgrn_w_out):
    bsz, _, d = x.shape
    depth = ada_w.shape[0]
    assert bsz + 1 <= MOD_ROWS

    c_rows = jnp.concatenate([c, c_ctx[None, :], jnp.zeros((MOD_ROWS - bsz - 1, d), F32)], axis=0)
    mod = _adaln(c_rows, ada_w, ada_b).reshape(depth, MOD_ROWS, N_MOD, d)

    ffn_w_in = ffn_w_in.astype(BF16)
    ffn_w_out = ffn_w_out.astype(BF16)
    fourier_w_out = fourier_w_out.astype(BF16)
    hgrn_w_in = hgrn_w_in.astype(BF16)
    hgrn_w_out = hgrn_w_out.astype(BF16)

    for i in range(depth):
        last = i == depth - 1
        is_hgrn = i % 2 == 1
        jm = i // 2
        mx = lambda j: mod[i, :bsz, 3 * j:3 * j + 3]
        mc = lambda j: mod[i, bsz:bsz + 1, 3 * j:3 * j + 3]
        gpre = lambda j: norm_pre[i, j][None, :]
        gpost = lambda j: norm_post[i, j][None, :]
        ctx_mod = lambda j: jnp.broadcast_to(mc(j), (bsz, 3, d))

        x = _ffn(x, mx(0), gpre(0), gpost(0), ffn_w_in, ffn_w_out, (i, 0))
        if is_hgrn or not last:
            ctx = _ffn(ctx, ctx_mod(0), gpre(0), gpost(0), ffn_w_in, ffn_w_out, (i, 0))

        if not is_hgrn:
            x = _fourier_latent(x, mx(1), gpre(1), gpost(1), fourier_w_out, (jm,))
            if not last:
                ctx = _fourier_ctx(ctx, mc(1), gpre(1), gpost(1), fourier_w_out, (jm,))
        else:
            hd = d // HGRN_HEADS
            zero = jnp.zeros((bsz, HGRN_HEADS, hd, hd), F32)
            qc, vc, kfc, bfc, kbc, bbc, sgc = _hgrn_inputs(
                ctx, ctx_mod(1), gpre(1), hgrn_w_in, (jm,), hgrn_lb_fwd, hgrn_lb_bwd, jm)
            ocf, s_f = _scan(qc, kfc, vc, bfc, zero, False)
            ocb, s_b = _scan(qc, kbc, vc, bbc, zero, True)
            qx, vx, kfx, bfx, kbx, bbx, sgx = _hgrn_inputs(
                x, mx(1), gpre(1), hgrn_w_in, (jm,), hgrn_lb_fwd, hgrn_lb_bwd, jm)
            oxf, _ = _scan(qx, kfx, vx, bfx, s_f, False)
            oxb, _ = _scan(qx, kbx, vx, bbx, s_b, True)
            gn = hgrn_norm[jm][None, :]
            x = _hgrn_readout(oxf, oxb, sgx, x, mx(1), gpost(1), gn, hgrn_w_out, (jm,))
            if not last:
                ctx = _hgrn_readout(ocf, ocb, sgc, ctx, ctx_mod(1), gpost(1), gn, hgrn_w_out, (jm,))

        x = _ffn(x, mx(2), gpre(2), gpost(2), ffn_w_in, ffn_w_out, (i, 1))
        if not last:
            ctx = _ffn(ctx, ctx_mod(2), gpre(2), gpost(2), ffn_w_in, ffn_w_out, (i, 1))
    return x
```

```python
import functools

import jax
import jax.numpy as jnp
import numpy as np
from jax import lax
from jax.experimental import pallas as pl
from jax.experimental.pallas import tpu as pltpu

F32 = jnp.float32
BF16 = jnp.bfloat16

GRID_W = 64
FOURIER_GROUPS = 4
HGRN_HEADS = 8
N_MOD = 9
HALF = 0.5
NORM_EPS = 1e-6
LB_FLOOR = 1e-30

VMEM_LIMIT_V7X = 56 * 1024 * 1024
SUBLANES = 8
LANES = 128
MXU_COLS = 256
SCAN_TILE = 512
FAST_BLOCKS = (64, 32, 16)
FAST_RANGE = 110.0
SCAN_HEADS = 8
MOD_ROWS = 16


def _params(*sem):
    return pltpu.CompilerParams(dimension_semantics=sem, vmem_limit_bytes=VMEM_LIMIT_V7X)


def _const_spec(shape):
    n = len(shape)
    return pl.BlockSpec(shape, lambda *_: (0,) * n, pipeline_mode=pl.Buffered(1))


def _pick_spec(stacked, lead):
    tail = stacked.shape[len(lead):]
    return pl.BlockSpec((None,) * len(lead) + tail, lambda *_: tuple(lead) + (0,) * len(tail),
                        pipeline_mode=pl.Buffered(1))


def _rms(x, g):
    ms = jnp.mean(x * x, axis=-1, keepdims=True)
    return x * lax.rsqrt(ms + NORM_EPS) * g


def _pre(x, mod_ref, g_ref):
    return _rms(x, g_ref[...]) * (1.0 + mod_ref[1:2, :]) + mod_ref[0:1, :]


def _post(x, y, mod_ref, g_ref, w):
    return x + w * mod_ref[2:3, :] * _rms(y, g_ref[...])


def _silu(x):
    return x * jax.nn.sigmoid(x)


def _dot(a, b):
    return jnp.dot(a, b, preferred_element_type=F32)


def _dot_nt(a, b):
    return lax.dot_general(a, b, (((1,), (1,)), ((), ())), preferred_element_type=F32)


def _dot_tn(a, b):
    return lax.dot_general(a, b, (((0,), (0,)), ((), ())), preferred_element_type=F32)


def _adaln_kernel(c_ref, w_ref, b_ref, o_ref):
    sc = _silu(c_ref[...]).astype(BF16)
    o_ref[...] = _dot(sc, w_ref[...].astype(BF16)) + b_ref[...]


def _adaln(c_rows, ada_w, ada_b):
    depth, d, n = ada_w.shape
    tn = 1024
    return pl.pallas_call(
        _adaln_kernel,
        grid=(depth, n // tn),
        in_specs=[
            pl.BlockSpec((MOD_ROWS, d), lambda i, j: (0, 0)),
            pl.BlockSpec((None, d, tn), lambda i, j: (i, 0, j)),
            pl.BlockSpec((None, 1, tn), lambda i, j: (i, 0, j)),
        ],
        out_specs=pl.BlockSpec((None, MOD_ROWS, tn), lambda i, j: (i, 0, j)),
        out_shape=jax.ShapeDtypeStruct((depth, MOD_ROWS, n), F32),
        compiler_params=_params("parallel", "parallel"),
        name="adaln",
    )(c_rows, ada_w, ada_b.reshape(depth, 1, n))


def _ffn_step(x, mod_ref, gpre_ref, gpost_ref, win_ref, wout_ref):
    d_ff = wout_ref.shape[0]
    hb = _pre(x, mod_ref, gpre_ref).astype(BF16)
    fc = MXU_COLS
    acc = None
    for ci in range(d_ff // fc):
        gate = _dot(hb, win_ref[:, ci * fc:(ci + 1) * fc])
        up = _dot(hb, win_ref[:, d_ff + ci * fc:d_ff + (ci + 1) * fc])
        a = (_silu(gate) * up).astype(BF16)
        y = _dot(a, wout_ref[ci * fc:(ci + 1) * fc, :])
        acc = y if acc is None else acc + y
    return _post(x, acc, mod_ref, gpost_ref, HALF)


def _ffn_kernel(x_ref, mod_ref, gpre_ref, gpost_ref, win_ref, wout_ref, o_ref):
    o_ref[...] = _ffn_step(x_ref[...], mod_ref, gpre_ref, gpost_ref, win_ref, wout_ref)


def _token_tile(n_tokens, want):
    return want if n_tokens % want == 0 else n_tokens


def _ffn(s, mod, gpre, gpost, w_in, w_out, lead):
    b, l, d = s.shape
    d_ff = w_out.shape[-2]
    assert d_ff % MXU_COLS == 0
    tm = _token_tile(l, 1024)
    return pl.pallas_call(
        _ffn_kernel,
        grid=(b, l // tm),
        in_specs=[
            pl.BlockSpec((None, tm, d), lambda i, j: (i, j, 0)),
            pl.BlockSpec((None, 3, d), lambda i, j: (i, 0, 0)),
            _const_spec((1, d)),
            _const_spec((1, d)),
            _pick_spec(w_in, lead),
            _pick_spec(w_out, lead),
        ],
        out_specs=pl.BlockSpec((None, tm, d), lambda i, j: (i, j, 0)),
        out_shape=jax.ShapeDtypeStruct(s.shape, F32),
        compiler_params=_params("parallel", "parallel"),
        name="ffn",
    )(s, mod, gpre, gpost, w_in, w_out)


def _dft_cos_sin(n):
    k = np.arange(n)
    ang = 2.0 * np.pi * ((k[:, None] * k[None, :]) % n) / n
    s = 1.0 / np.sqrt(n)
    return np.cos(ang) * s, np.sin(ang) * s


def _fourier_consts(gd):
    c, s = _dft_cos_sin(gd)
    chan = np.concatenate([c, s], axis=1)
    c, s = _dft_cos_sin(GRID_W)
    col = np.block([[c, -s], [s, c]])
    return jnp.asarray(chan, BF16), jnp.asarray(col, BF16)


def _channel_dft(hb, chan_ref, gd):
    us, vs = [], []
    for g in range(FOURIER_GROUPS):
        uv = _dot(hb[:, g * gd:(g + 1) * gd], chan_ref[...])
        us.append(uv[:, :gd])
        vs.append(uv[:, gd:])
    return (jnp.concatenate(us, axis=1).astype(BF16), jnp.concatenate(vs, axis=1).astype(BF16))


def _pack_pair(a, b):
    hi = lax.bitcast_convert_type(a.astype(BF16).astype(F32), jnp.uint32)
    lo = lax.bitcast_convert_type(b.astype(BF16).astype(F32), jnp.uint32)
    return hi | (lo >> 16)


def _unpack_pair(p):
    a = lax.bitcast_convert_type(p & jnp.uint32(0xFFFF0000), F32)
    b = lax.bitcast_convert_type(p << 16, F32)
    return a.astype(BF16), b.astype(BF16)


def _fourier_cols_kernel(x_ref, mod_ref, gpre_ref, chan_ref, col_ref, ab_ref, *, rows_per_step, gd):
    hb = _pre(x_ref[...], mod_ref, gpre_ref).astype(BF16)
    u, v = _channel_dft(hb, chan_ref, gd)
    for r in range(rows_per_step):
        sl = slice(r * GRID_W, (r + 1) * GRID_W)
        ab = _dot(col_ref[...], jnp.concatenate([u[sl], v[sl]], axis=0))
        ab_ref[sl, :] = _pack_pair(ab[:GRID_W], ab[GRID_W:])


def _fourier_rows_kernel(ab_ref, x_ref, mod_ref, gpost_ref, rowk_ref, w_ref, o_ref):
    rows, cps, d = x_ref.shape
    flat = lambda v: v.reshape(rows * cps, d)
    a, b = _unpack_pair(flat(ab_ref[...]))
    y = _dot(rowk_ref[...], jnp.concatenate([a, b], axis=0)).astype(BF16)
    res = _post(flat(x_ref[...]), _dot(y, w_ref[...]), mod_ref, gpost_ref, 1.0)
    o_ref[...] = res.reshape(rows, cps, d)


def _fourier_latent(x, mod, gpre, gpost, w_out, lead):
    b, l, d = x.shape
    rows = l // GRID_W
    gd = d // FOURIER_GROUPS
    chan, col = _fourier_consts(gd)
    rps = 8 if rows % 8 == 0 else 1
    tm = rps * GRID_W
    ab = pl.pallas_call(
        functools.partial(_fourier_cols_kernel, rows_per_step=rps, gd=gd),
        grid=(b, l // tm),
        in_specs=[
            pl.BlockSpec((None, tm, d), lambda i, j: (i, j, 0)),
            pl.BlockSpec((None, 3, d), lambda i, j: (i, 0, 0)),
            _const_spec((1, d)),
            _const_spec(chan.shape),
            _const_spec(col.shape),
        ],
        out_specs=pl.BlockSpec((None, tm, d), lambda i, j: (i, j, 0)),
        out_shape=jax.ShapeDtypeStruct((b, l, d), jnp.uint32),
        compiler_params=_params("parallel", "parallel"),
        name="fourier_cols",
    )(x, mod, gpre, chan, col)

    c, s = _dft_cos_sin(rows)
    cps = SUBLANES
    eye = np.eye(cps)
    rowk = jnp.asarray(np.concatenate([np.kron(c, eye), -np.kron(s, eye)], axis=1), BF16)
    grid_view = lambda t: t.reshape(b, rows, GRID_W, d)
    blk = pl.BlockSpec((None, rows, cps, d), lambda i, j: (i, 0, j, 0))
    out = pl.pallas_call(
        _fourier_rows_kernel,
        grid=(b, GRID_W // cps),
        in_specs=[
            blk, blk,
            pl.BlockSpec((None, 3, d), lambda i, j: (i, 0, 0)),
            _const_spec((1, d)),
            _const_spec(rowk.shape),
            _pick_spec(w_out, lead),
        ],
        out_specs=blk,
        out_shape=jax.ShapeDtypeStruct((b, rows, GRID_W, d), F32),
        compiler_params=_params("parallel", "parallel"),
        name="fourier_rows",
    )(grid_view(ab), grid_view(x), mod, gpost, rowk, w_out)
    return out.reshape(b, l, d)


def _fourier_ctx_kernel(x_ref, mod_ref, gpre_ref, gpost_ref, chan_ref, seq_ref, w_ref, o_ref, *, gd):
    x = x_ref[...]
    hb = _pre(x, mod_ref, gpre_ref).astype(BF16)
    u, v = _channel_dft(hb, chan_ref, gd)
    y = _dot(seq_ref[...], jnp.concatenate([u, v], axis=0)).astype(BF16)
    o_ref[...] = _post(x, _dot(y, w_ref[...]), mod_ref, gpost_ref, 1.0)


def _fourier_ctx(x, mod, gpre, gpost, w_out, lead):
    b, l, d = x.shape
    gd = d // FOURIER_GROUPS
    chan, _ = _fourier_consts(gd)
    c, s = _dft_cos_sin(l)
    seqm = jnp.asarray(np.concatenate([c, -s], axis=1), BF16)
    return pl.pallas_call(
        functools.partial(_fourier_ctx_kernel, gd=gd),
        grid=(b,),
        in_specs=[
            pl.BlockSpec((None, l, d), lambda i: (i, 0, 0)),
            pl.BlockSpec((None, 3, d), lambda i: (0, 0, 0)),
            _const_spec((1, d)),
            _const_spec((1, d)),
            _const_spec(chan.shape),
            _const_spec(seqm.shape),
            _pick_spec(w_out, lead),
        ],
        out_specs=pl.BlockSpec((None, l, d), lambda i: (i, 0, 0)),
        out_shape=jax.ShapeDtypeStruct(x.shape, F32),
        compiler_params=_params("parallel"),
        name="fourier_ctx",
    )(x, mod, gpre, gpost, chan, seqm, w_out)


def _lower_bound(logits_ref, j):
    lg = logits_ref[...]
    e = jnp.exp(lg - jnp.max(lg, axis=0, keepdims=True))
    p = e / jnp.sum(e, axis=0, keepdims=True)
    lb = jnp.zeros_like(p[0:1])
    for i in range(1, j + 1):
        lb = lb + p[i:i + 1]
    return lb


def _forget_gate(z, lb):
    e = jnp.exp(-jnp.abs(z))
    inv = 1.0 / (1.0 + e)
    pos = z >= 0.0
    sig = jnp.where(pos, inv, e * inv)
    nsig = jnp.where(pos, e * inv, inv)
    f = jnp.maximum(lb, LB_FLOOR) + (1.0 - lb) * sig
    return (1.0 - lb) * nsig, jnp.log2(f)


def _split2(x):
    hi = x.astype(BF16)
    lo = (x - hi.astype(F32)).astype(BF16)
    return hi, lo


def _tile_cumsum(g, tri_ref, tile, reverse):
    c = tri_ref.shape[0]
    parts = _split2(g)
    chunks = [sum(_dot(tri_ref[...], p[i:i + c]) for p in parts) for i in range(0, g.shape[0], c)]
    per_tile = tile // c
    order = range(per_tile - 1, -1, -1) if reverse else range(per_tile)
    last = 0 if reverse else c - 1
    for t0 in range(0, len(chunks), per_tile):
        carry = None
        for i in order:
            if carry is not None:
                chunks[t0 + i] = chunks[t0 + i] + carry
            carry = chunks[t0 + i][last:last + 1, :]
    return jnp.concatenate(chunks, axis=0)


def _hgrn_in_kernel(x_ref, mod_ref, gpre_ref, w_ref, lbf_ref, lbb_ref, trif_ref, trib_ref,
                    q_ref, v_ref, kf_ref, bf_ref, kb_ref, bb_ref, sg_ref, *, layer_j, kd, hd, tile):
    hb = _pre(x_ref[...], mod_ref, gpre_ref).astype(BF16)
    col = lambda n: _dot(hb, w_ref[:, n * kd:(n + 1) * kd])
    q = _silu(col(0))
    v = col(1)
    kf, lf = _forget_gate(col(2), _lower_bound(lbf_ref, layer_j))
    kb, lbw = _forget_gate(col(3), _lower_bound(lbb_ref, layer_j))
    sg_ref[...] = _silu(col(4)).astype(BF16)
    bf = _tile_cumsum(lf, trif_ref, tile, False)
    bb = _tile_cumsum(lbw, trib_ref, tile, True)
    for h in range(HGRN_HEADS):
        sl = slice(h * hd, (h + 1) * hd)
        q_ref[h] = q[:, sl].astype(BF16)
        v_ref[h] = v[:, sl].astype(BF16)
        kf_ref[h] = kf[:, sl].astype(BF16)
        kb_ref[h] = kb[:, sl].astype(BF16)
        bf_ref[h] = bf[:, sl]
        bb_ref[h] = bb[:, sl]


def _hgrn_inputs(x, mod, gpre, w_in, lead, lb_fwd, lb_bwd, layer_j):
    b, l, d = x.shape
    kd = d
    hd = kd // HGRN_HEADS
    tm = _token_tile(l, 512)
    tile = _token_tile(tm, SCAN_TILE)
    t = _token_tile(tile, MXU_COLS)
    trif = _scan_consts(t, False)[0]
    trib = _scan_consts(t, True)[0]
    x_spec = pl.BlockSpec((None, tm, d), lambda i, j: (i, j, 0))
    head_spec = pl.BlockSpec((None, HGRN_HEADS, tm, hd), lambda i, j: (i, 0, j, 0))
    heads = lambda dt: jax.ShapeDtypeStruct((b, HGRN_HEADS, l, hd), dt)
    return pl.pallas_call(
        functools.partial(_hgrn_in_kernel, layer_j=layer_j, kd=kd, hd=hd, tile=tile),
        grid=(b, l // tm),
        in_specs=[
            x_spec,
            pl.BlockSpec((None, 3, d), lambda i, j: (i, 0, 0)),
            _const_spec((1, d)),
            _pick_spec(w_in, lead),
            _const_spec(lb_fwd.shape),
            _const_spec(lb_bwd.shape),
            _const_spec((t, t)),
            _const_spec((t, t)),
        ],
        out_specs=[head_spec] * 6 + [x_spec],
        out_shape=[heads(BF16), heads(BF16), heads(BF16), heads(F32), heads(BF16), heads(F32),
                   jax.ShapeDtypeStruct((b, l, d), BF16)],
        compiler_params=_params("parallel", "parallel"),
        name="hgrn_in",
    )(x, mod, gpre, w_in, lb_fwd, lb_bwd, trif, trib)


def _scan_consts(n, reverse):
    r = np.arange(n)[:, None]
    c = np.arange(n)[None, :]
    valid = (c >= r) if reverse else (c <= r)
    x = r ^ c
    lvl = np.where(x == 0, 0, np.floor(np.log2(np.maximum(x, 1))).astype(np.int64) + 1)
    fine = np.where(valid & (x < SUBLANES), lvl, -1)
    gap = np.where(valid, x, n)
    return jnp.asarray(valid, BF16), jnp.asarray(fine, jnp.int32), jnp.asarray(gap, jnp.int32)


def _level_ref(b, m, reverse):
    t, n = b.shape
    first = m if reverse else m - 1
    b3 = b.reshape(t // SUBLANES, SUBLANES, n)
    rid = lax.broadcasted_iota(jnp.int32, b3.shape, 1)
    ref = None
    for p in range(0, SUBLANES, 2 * m):
        row = jnp.broadcast_to(b3[:, p + first:p + first + 1, :], b3.shape)
        ref = row if ref is None else jnp.where(rid >= p, row, ref)
    return ref.reshape(t, n)


def _neg_abs(d):
    bits = lax.bitcast_convert_type(d, jnp.int32) | jnp.int32(-2 ** 31)
    return lax.bitcast_convert_type(bits, F32)


def _scaled(z, e):
    return (z * e).astype(BF16)


def _halves(lo, m, reverse):
    return (lo, lo + m, lo + m) if reverse else (lo + m, lo, lo + m - 1)


def _strip_scores(q, k, q32, k32, b, fine, gap, lane, reverse, fast):
    n = q.shape[0]
    if fast:
        first = fast - 1 if reverse else 0
        ref = jnp.concatenate([jnp.broadcast_to(b[i + first:i + first + 1, :], (fast, b.shape[1]))
                               for i in range(0, n, fast)], axis=0)
        d = b - ref
        a = jnp.where(gap < fast, _dot_nt(_scaled(q32, jnp.exp2(d)), _scaled(k32, jnp.exp2(-d))), 0.0)
        m = fast
    else:
        a = jnp.where(fine == 0, _dot_nt(q, k), 0.0)
        before = pltpu.roll(b, (n - 1) if reverse else 1, 0)
        a = jnp.where(fine == 1, _dot_nt(_scaled(q32, jnp.exp2(_neg_abs(b - before))), k), a)
        m, level = 2, 2
        while m < SUBLANES:
            e = jnp.exp2(_neg_abs(b - _level_ref(b, m, reverse)))
            a = jnp.where(fine == level, _dot_nt(_scaled(q32, e), _scaled(k32, e)), a)
            m, level = 2 * m, level + 1
    blocks = [a[i:i + SUBLANES] for i in range(0, n, SUBLANES)]
    while m < n:
        ql, kf = [], []
        for lo in range(0, n, 2 * m):
            late, early, first = _halves(lo, m, reverse)
            ref = b[first:first + 1, :]
            ql.append(_scaled(q32[late:late + m], jnp.exp2(b[late:late + m] - ref)))
            ke = _scaled(k32[early:early + m], jnp.exp2(ref - b[early:early + m]))
            kf += [k[late:late + m], ke] if reverse else [ke, k[late:late + m]]
        s_m = _dot_nt(jnp.concatenate(ql, axis=0), jnp.concatenate(kf, axis=0))
        for pi, lo in enumerate(range(0, n, 2 * m)):
            late, early, _ = _halves(lo, m, reverse)
            inside = (lane >= early) & (lane < early + m)
            for i in range(0, m, SUBLANES):
                rb = (late + i) // SUBLANES
                blocks[rb] = jnp.where(inside, s_m[pi * m + i:pi * m + i + SUBLANES], blocks[rb])
        m *= 2
    return jnp.concatenate(blocks, axis=0)


def _scan_tile(q, k, v, b, st, fine, gap, reverse, fast):
    t = q.shape[0]
    n = fine.shape[0]
    q32 = q.astype(F32)
    k32 = k.astype(F32)
    end = 0 if reverse else t - 1
    b_end = b[end:end + 1, :]

    o = _dot_nt(_scaled(q32, jnp.exp2(b)), st.astype(BF16))
    new_st = st * jnp.exp2(b_end) + _dot_tn(v, _scaled(k32, jnp.exp2(b_end - b)))

    lane = lax.broadcasted_iota(jnp.int32, (SUBLANES, n), 1)
    strips = range(0, t, n)
    score = {(i, i): _strip_scores(q[i:i + n], k[i:i + n], q32[i:i + n], k32[i:i + n], b[i:i + n],
                                   fine, gap, lane, reverse, fast) for i in strips}
    m = n
    while m < t:
        for lo in range(0, t, 2 * m):
            late, early, first = _halves(lo, m, reverse)
            ref = b[first:first + 1, :]
            ql = _scaled(q32[late:late + m], jnp.exp2(b[late:late + m] - ref))
            ke = _scaled(k32[early:early + m], jnp.exp2(ref - b[early:early + m]))
            s_m = _dot_nt(ql, ke)
            for i in range(0, m, n):
                for j in range(0, m, n):
                    score[(late + i, early + j)] = s_m[i:i + n, j:j + n]
        m *= 2
    outs = []
    for i in strips:
        keys = [j for j in strips if (i, j) in score]
        a = jnp.concatenate([score[(i, j)] for j in keys], axis=1).astype(BF16)
        outs.append(_dot(a, jnp.concatenate([v[j:j + n] for j in keys], axis=0)))
    return o + jnp.concatenate(outs, axis=0), new_st


def _scan_kernel(q_ref, k_ref, v_ref, b_ref, s0_ref, fine_ref, gap_ref, o_ref, sfin_ref, state_ref,
                 *, reverse, n_tiles):
    step = pl.program_id(2)

    @pl.when(step == 0)
    def _():
        state_ref[...] = s0_ref[...]

    hp, t, _ = q_ref.shape

    def spread(blk):
        worst = None
        for h in range(hp):
            top = b_ref[h, pl.ds(0, t // blk, stride=blk), :]
            bottom = b_ref[h, pl.ds(blk - 1, t // blk, stride=blk), :]
            s = jnp.abs(top - bottom)
            worst = s if worst is None else jnp.maximum(worst, s)
        return jnp.max(worst)

    def run(fast):
        fine = fine_ref[...]
        gap = gap_ref[...]
        for h in range(hp):
            o, st = _scan_tile(q_ref[h], k_ref[h], v_ref[h], b_ref[h], state_ref[h], fine, gap, reverse, fast)
            state_ref[h] = st
            o_ref[h] = o.astype(o_ref.dtype)

    blocks = [blk for blk in FAST_BLOCKS if t % blk == 0]
    fits = [spread(blk) <= FAST_RANGE for blk in blocks]
    unresolved = None
    for blk, ok in zip(blocks, fits):
        pl.when(ok if unresolved is None else unresolved & ok)(functools.partial(run, blk))
        unresolved = jnp.logical_not(ok) if unresolved is None else unresolved & jnp.logical_not(ok)
    if unresolved is None:
        run(0)
    else:
        pl.when(unresolved)(functools.partial(run, 0))

    @pl.when(step == n_tiles - 1)
    def _():
        sfin_ref[...] = state_ref[...]


def _scan(q, k, v, bcum, s0, reverse):
    b, h, l, hd = q.shape
    t = _token_tile(l, SCAN_TILE)
    n_tiles = l // t
    hp = SCAN_HEADS
    _, fine, gap = _scan_consts(LANES, reverse)
    tile = (lambda j: n_tiles - 1 - j) if reverse else (lambda j: j)
    tok_spec = pl.BlockSpec((None, hp, t, hd), lambda i, hh, j: (i, hh, tile(j), 0))
    st_spec = pl.BlockSpec((None, hp, hd, hd), lambda i, hh, j: (i, hh, 0, 0))
    return pl.pallas_call(
        functools.partial(_scan_kernel, reverse=reverse, n_tiles=n_tiles),
        grid=(b, h // hp, n_tiles),
        in_specs=[tok_spec, tok_spec, tok_spec, tok_spec, st_spec, _const_spec(fine.shape), _const_spec(gap.shape)],
        out_specs=[tok_spec, st_spec],
        out_shape=[jax.ShapeDtypeStruct((b, h, l, hd), BF16), jax.ShapeDtypeStruct((b, h, hd, hd), F32)],
        scratch_shapes=[pltpu.VMEM((hp, hd, hd), F32)],
        compiler_params=_params("parallel", "parallel", "arbitrary"),
        name="scan_bwd" if reverse else "scan_fwd",
    )(q, k, v, bcum, s0, fine, gap)


def _hgrn_out_ffn_kernel(of_ref, ob_ref, sg_ref, x_ref, mod_ref, gpost_ref, gn_ref, w_ref,
                         fmod_ref, fgpre_ref, fgpost_ref, win_ref, wout_ref, o_ref):
    heads = []
    for h in range(HGRN_HEADS):
        heads.append(_rms(of_ref[h].astype(F32) + ob_ref[h].astype(F32), gn_ref[...]))
    o = jnp.concatenate(heads, axis=1)
    y = _dot((o * sg_ref[...].astype(F32)).astype(BF16), w_ref[...])
    x = _post(x_ref[...], y, mod_ref, gpost_ref, 1.0)
    o_ref[...] = _ffn_step(x, fmod_ref, fgpre_ref, fgpost_ref, win_ref, wout_ref)


def _hgrn_readout_ffn(o_f, o_b, sg, x, mod, gpost, g_norm, w_out, lead, fmod, fgpre, fgpost, w_in, w_ffn, flead):
    b, l, d = x.shape
    hd = d // HGRN_HEADS
    assert w_ffn.shape[-2] % MXU_COLS == 0
    tm = _token_tile(l, 512)
    x_spec = pl.BlockSpec((None, tm, d), lambda i, j: (i, j, 0))
    head_spec = pl.BlockSpec((None, HGRN_HEADS, tm, hd), lambda i, j: (i, 0, j, 0))
    mod_spec = pl.BlockSpec((None, 3, d), lambda i, j: (i, 0, 0))
    return pl.pallas_call(
        _hgrn_out_ffn_kernel,
        grid=(b, l // tm),
        in_specs=[
            head_spec, head_spec, x_spec, x_spec,
            mod_spec,
            _const_spec((1, d)),
            _const_spec((1, hd)),
            _pick_spec(w_out, lead),
            mod_spec,
            _const_spec((1, d)),
            _const_spec((1, d)),
            _pick_spec(w_in, flead),
            _pick_spec(w_ffn, flead),
        ],
        out_specs=x_spec,
        out_shape=jax.ShapeDtypeStruct(x.shape, F32),
        compiler_params=_params("parallel", "parallel"),
        name="hgrn_out_ffn",
    )(o_f, o_b, sg, x, mod, gpost, g_norm, w_out, fmod, fgpre, fgpost, w_in, w_ffn)


def kernel(x, c, ctx, c_ctx, ada_w, ada_b, norm_pre, norm_post, ffn_w_in, ffn_w_out, fourier_w_out,
           hgrn_w_in, hgrn_lb_fwd, hgrn_lb_bwd, hgrn_norm, hgrn_w_out):
    bsz, _, d = x.shape
    depth = ada_w.shape[0]
    assert bsz + 1 <= MOD_ROWS

    c_rows = jnp.concatenate([c, c_ctx[None, :], jnp.zeros((MOD_ROWS - bsz - 1, d), F32)], axis=0)
    mod = _adaln(c_rows, ada_w, ada_b).reshape(depth, MOD_ROWS, N_MOD, d)

    ffn_w_in = ffn_w_in.astype(BF16)
    ffn_w_out = ffn_w_out.astype(BF16)
    fourier_w_out = fourier_w_out.astype(BF16)
    hgrn_w_in = hgrn_w_in.astype(BF16)
    hgrn_w_out = hgrn_w_out.astype(BF16)

    for i in range(depth):
        last = i == depth - 1
        is_hgrn = i % 2 == 1
        jm = i // 2
        mx = lambda j: mod[i, :bsz, 3 * j:3 * j + 3]
        mc = lambda j: mod[i, bsz:bsz + 1, 3 * j:3 * j + 3]
        gpre = lambda j: norm_pre[i, j][None, :]
        gpost = lambda j: norm_post[i, j][None, :]
        ctx_mod = lambda j: jnp.broadcast_to(mc(j), (bsz, 3, d))

        x = _ffn(x, mx(0), gpre(0), gpost(0), ffn_w_in, ffn_w_out, (i, 0))
        if is_hgrn or not last:
            ctx = _ffn(ctx, ctx_mod(0), gpre(0), gpost(0), ffn_w_in, ffn_w_out, (i, 0))

        if not is_hgrn:
            x = _fourier_latent(x, mx(1), gpre(1), gpost(1), fourier_w_out, (jm,))
            if not last:
                ctx = _fourier_ctx(ctx, mc(1), gpre(1), gpost(1), fourier_w_out, (jm,))
        else:
            hd = d // HGRN_HEADS
            zero = jnp.zeros((bsz, HGRN_HEADS, hd, hd), F32)
            qc, vc, kfc, bfc, kbc, bbc, sgc = _hgrn_inputs(
                ctx, ctx_mod(1), gpre(1), hgrn_w_in, (jm,), hgrn_lb_fwd, hgrn_lb_bwd, jm)
            ocf, s_f = _scan(qc, kfc, vc, bfc, zero, False)
            ocb, s_b = _scan(qc, kbc, vc, bbc, zero, True)
            qx, vx, kfx, bfx, kbx, bbx, sgx = _hgrn_inputs(
                x, mx(1), gpre(1), hgrn_w_in, (jm,), hgrn_lb_fwd, hgrn_lb_bwd, jm)
            oxf, _ = _scan(qx, kfx, vx, bfx, s_f, False)
            oxb, _ = _scan(qx, kbx, vx, bbx, s_b, True)
            gn = hgrn_norm[jm][None, :]
            x = _hgrn_readout_ffn(oxf, oxb, sgx, x, mx(1), gpost(1), gn, hgrn_w_out, (jm,),
                                  mx(2), gpre(2), gpost(2), ffn_w_in, ffn_w_out, (i, 1))
            if not last:
                ctx = _hgrn_readout_ffn(ocf, ocb, sgc, ctx, ctx_mod(1), gpost(1), gn, hgrn_w_out, (jm,),
                                        ctx_mod(2), gpre(2), gpost(2), ffn_w_in, ffn_w_out, (i, 1))
            continue

        x = _ffn(x, mx(2), gpre(2), gpost(2), ffn_w_in, ffn_w_out, (i, 1))
        if not last:
            ctx = _ffn(ctx, ctx_mod(2), gpre(2), gpost(2), ffn_w_in, ffn_w_out, (i, 1))
    return x
```

```python
import functools

import jax
import jax.numpy as jnp
import numpy as np
from jax import lax
from jax.experimental import pallas as pl
from jax.experimental.pallas import tpu as pltpu

F32 = jnp.float32
BF16 = jnp.bfloat16

GRID_W = 64
FOURIER_GROUPS = 4
HGRN_HEADS = 8
N_MOD = 9
HALF = 0.5
NORM_EPS = 1e-6
LB_FLOOR = 1e-30

VMEM_LIMIT_V7X = 56 * 1024 * 1024
SUBLANES = 8
LANES = 128
MXU_COLS = 256
FFN_TOKENS = 1024
SCAN_TILE = 512
FAST_BLOCKS = (64, 32, 16)
FAST_RANGE = 110.0
SCAN_HEADS = 8
MOD_ROWS = 16


def _params(*sem):
    return pltpu.CompilerParams(dimension_semantics=sem, vmem_limit_bytes=VMEM_LIMIT_V7X)


def _const_spec(shape):
    n = len(shape)
    return pl.BlockSpec(shape, lambda *_: (0,) * n, pipeline_mode=pl.Buffered(1))


def _pick_spec(stacked, lead):
    tail = stacked.shape[len(lead):]
    return pl.BlockSpec((None,) * len(lead) + tail, lambda *_: tuple(lead) + (0,) * len(tail),
                        pipeline_mode=pl.Buffered(1))


def _rms(x, g):
    ms = jnp.mean(x * x, axis=-1, keepdims=True)
    return x * lax.rsqrt(ms + NORM_EPS) * g


def _pre(x, mod_ref, g_ref):
    return _rms(x, g_ref[...]) * (1.0 + mod_ref[1:2, :]) + mod_ref[0:1, :]


def _post(x, y, mod_ref, g_ref, w):
    return x + w * mod_ref[2:3, :] * _rms(y, g_ref[...])


def _silu(x):
    return x * jax.nn.sigmoid(x)


def _dot(a, b):
    return jnp.dot(a, b, preferred_element_type=F32)


def _dot_nt(a, b):
    return lax.dot_general(a, b, (((1,), (1,)), ((), ())), preferred_element_type=F32)


def _dot_tn(a, b):
    return lax.dot_general(a, b, (((0,), (0,)), ((), ())), preferred_element_type=F32)


def _adaln_kernel(c_ref, w_ref, b_ref, o_ref):
    sc = _silu(c_ref[...]).astype(BF16)
    o_ref[...] = _dot(sc, w_ref[...].astype(BF16)) + b_ref[...]


def _adaln(c_rows, ada_w, ada_b):
    depth, d, n = ada_w.shape
    tn = 1024
    return pl.pallas_call(
        _adaln_kernel,
        grid=(depth, n // tn),
        in_specs=[
            pl.BlockSpec((MOD_ROWS, d), lambda i, j: (0, 0)),
            pl.BlockSpec((None, d, tn), lambda i, j: (i, 0, j)),
            pl.BlockSpec((None, 1, tn), lambda i, j: (i, 0, j)),
        ],
        out_specs=pl.BlockSpec((None, MOD_ROWS, tn), lambda i, j: (i, 0, j)),
        out_shape=jax.ShapeDtypeStruct((depth, MOD_ROWS, n), F32),
        compiler_params=_params("parallel", "parallel"),
        name="adaln",
    )(c_rows, ada_w, ada_b.reshape(depth, 1, n))


def _ffn_step(x, mod_ref, gpre_ref, gpost_ref, win_ref, wout_ref):
    d_ff = wout_ref.shape[0]
    hb = _pre(x, mod_ref, gpre_ref).astype(BF16)
    fc = MXU_COLS
    acc = None
    for ci in range(d_ff // fc):
        gate = _dot(hb, win_ref[:, ci * fc:(ci + 1) * fc])
        up = _dot(hb, win_ref[:, d_ff + ci * fc:d_ff + (ci + 1) * fc])
        a = (_silu(gate) * up).astype(BF16)
        y = _dot(a, wout_ref[ci * fc:(ci + 1) * fc, :])
        acc = y if acc is None else acc + y
    return _post(x, acc, mod_ref, gpost_ref, HALF)


def _ffn_kernel(x_ref, mod_ref, gpre_ref, gpost_ref, win_ref, wout_ref, o_ref):
    o_ref[...] = _ffn_step(x_ref[...], mod_ref, gpre_ref, gpost_ref, win_ref, wout_ref)


def _token_tile(n_tokens, want):
    return want if n_tokens % want == 0 else n_tokens


def _ffn(s, mod, gpre, gpost, w_in, w_out, lead):
    b, l, d = s.shape
    d_ff = w_out.shape[-2]
    assert d_ff % MXU_COLS == 0
    tm = _token_tile(l, FFN_TOKENS)
    return pl.pallas_call(
        _ffn_kernel,
        grid=(b, l // tm),
        in_specs=[
            pl.BlockSpec((None, tm, d), lambda i, j: (i, j, 0)),
            pl.BlockSpec((None, 3, d), lambda i, j: (i, 0, 0)),
            _const_spec((1, d)),
            _const_spec((1, d)),
            _pick_spec(w_in, lead),
            _pick_spec(w_out, lead),
        ],
        out_specs=pl.BlockSpec((None, tm, d), lambda i, j: (i, j, 0)),
        out_shape=jax.ShapeDtypeStruct(s.shape, F32),
        compiler_params=_params("parallel", "parallel"),
        name="ffn",
    )(s, mod, gpre, gpost, w_in, w_out)


def _dft_cos_sin(n):
    k = np.arange(n)
    ang = 2.0 * np.pi * ((k[:, None] * k[None, :]) % n) / n
    s = 1.0 / np.sqrt(n)
    return np.cos(ang) * s, np.sin(ang) * s


def _fourier_consts(gd):
    c, s = _dft_cos_sin(gd)
    chan = np.concatenate([c, s], axis=1)
    c, s = _dft_cos_sin(GRID_W)
    col = np.block([[c, -s], [s, c]])
    return jnp.asarray(chan, BF16), jnp.asarray(col, BF16)


def _channel_dft(hb, chan_ref, gd):
    us, vs = [], []
    for g in range(FOURIER_GROUPS):
        uv = _dot(hb[:, g * gd:(g + 1) * gd], chan_ref[...])
        us.append(uv[:, :gd])
        vs.append(uv[:, gd:])
    return (jnp.concatenate(us, axis=1).astype(BF16), jnp.concatenate(vs, axis=1).astype(BF16))


def _pack_pair(a, b):
    hi = lax.bitcast_convert_type(a.astype(BF16).astype(F32), jnp.uint32)
    lo = lax.bitcast_convert_type(b.astype(BF16).astype(F32), jnp.uint32)
    return hi | (lo >> 16)


def _unpack_pair(p):
    a = lax.bitcast_convert_type(p & jnp.uint32(0xFFFF0000), F32)
    b = lax.bitcast_convert_type(p << 16, F32)
    return a.astype(BF16), b.astype(BF16)


def _ffn_fourier_cols_kernel(x_ref, fmod_ref, fgpre_ref, fgpost_ref, win_ref, wout_ref,
                             mod_ref, gpre_ref, chan_ref, col_ref, o_ref, ab_ref, *, gd):
    x = _ffn_step(x_ref[...], fmod_ref, fgpre_ref, fgpost_ref, win_ref, wout_ref)
    o_ref[...] = x
    hb = _pre(x, mod_ref, gpre_ref).astype(BF16)
    u, v = _channel_dft(hb, chan_ref, gd)
    for r in range(x.shape[0] // GRID_W):
        sl = slice(r * GRID_W, (r + 1) * GRID_W)
        ab = _dot(col_ref[...], jnp.concatenate([u[sl], v[sl]], axis=0))
        ab_ref[sl, :] = _pack_pair(ab[:GRID_W], ab[GRID_W:])


def _fourier_rows_ffn_kernel(ab_ref, x_ref, mod_ref, gpost_ref, rowk_ref, w_ref,
                             fmod_ref, fgpre_ref, fgpost_ref, win_ref, wout_ref, o_ref):
    rows, cps, d = x_ref.shape
    flat = lambda v: v.reshape(rows * cps, d)
    a, b = _unpack_pair(flat(ab_ref[...]))
    y = _dot(rowk_ref[...], jnp.concatenate([a, b], axis=0)).astype(BF16)
    x = _post(flat(x_ref[...]), _dot(y, w_ref[...]), mod_ref, gpost_ref, 1.0)
    o_ref[...] = _ffn_step(x, fmod_ref, fgpre_ref, fgpost_ref, win_ref, wout_ref).reshape(rows, cps, d)


def _ffn_fourier_latent(x, fmod, fgpre, fgpost, w_in, w_ffn, lead_a, lead_b, mod, gpre, gpost, w_out, lead):
    b, l, d = x.shape
    rows = l // GRID_W
    gd = d // FOURIER_GROUPS
    assert w_ffn.shape[-2] % MXU_COLS == 0
    chan, col = _fourier_consts(gd)
    tm = _token_tile(l, 512)
    assert tm % GRID_W == 0
    tok = pl.BlockSpec((None, tm, d), lambda i, j: (i, j, 0))
    mod_spec = pl.BlockSpec((None, 3, d), lambda i, j: (i, 0, 0))
    x, ab = pl.pallas_call(
        functools.partial(_ffn_fourier_cols_kernel, gd=gd),
        grid=(b, l // tm),
        in_specs=[
            tok, mod_spec, _const_spec((1, d)), _const_spec((1, d)),
            _pick_spec(w_in, lead_a), _pick_spec(w_ffn, lead_a),
            mod_spec, _const_spec((1, d)), _const_spec(chan.shape), _const_spec(col.shape),
        ],
        out_specs=[tok, tok],
        out_shape=[jax.ShapeDtypeStruct((b, l, d), F32), jax.ShapeDtypeStruct((b, l, d), jnp.uint32)],
        compiler_params=_params("parallel", "parallel"),
        name="ffn_fourier_cols",
    )(x, fmod[0], fgpre[0], fgpost[0], w_in, w_ffn, mod, gpre, chan, col)

    c, s = _dft_cos_sin(rows)
    cps = SUBLANES
    eye = np.eye(cps)
    rowk = jnp.asarray(np.concatenate([np.kron(c, eye), -np.kron(s, eye)], axis=1), BF16)
    grid_view = lambda t: t.reshape(b, rows, GRID_W, d)
    blk = pl.BlockSpec((None, rows, cps, d), lambda i, j: (i, 0, j, 0))
    out = pl.pallas_call(
        _fourier_rows_ffn_kernel,
        grid=(b, GRID_W // cps),
        in_specs=[
            blk, blk, mod_spec, _const_spec((1, d)), _const_spec(rowk.shape), _pick_spec(w_out, lead),
            mod_spec, _const_spec((1, d)), _const_spec((1, d)),
            _pick_spec(w_in, lead_b), _pick_spec(w_ffn, lead_b),
        ],
        out_specs=blk,
        out_shape=jax.ShapeDtypeStruct((b, rows, GRID_W, d), F32),
        compiler_params=_params("parallel", "parallel"),
        name="fourier_rows_ffn",
    )(grid_view(ab), grid_view(x), mod, gpost, rowk, w_out, fmod[1], fgpre[1], fgpost[1], w_in, w_ffn)
    return out.reshape(b, l, d)


def _fourier_ctx_kernel(x_ref, mod_ref, gpre_ref, gpost_ref, chan_ref, seq_ref, w_ref, o_ref, *, gd):
    x = x_ref[...]
    hb = _pre(x, mod_ref, gpre_ref).astype(BF16)
    u, v = _channel_dft(hb, chan_ref, gd)
    y = _dot(seq_ref[...], jnp.concatenate([u, v], axis=0)).astype(BF16)
    o_ref[...] = _post(x, _dot(y, w_ref[...]), mod_ref, gpost_ref, 1.0)


def _fourier_ctx(x, mod, gpre, gpost, w_out, lead):
    b, l, d = x.shape
    gd = d // FOURIER_GROUPS
    chan, _ = _fourier_consts(gd)
    c, s = _dft_cos_sin(l)
    seqm = jnp.asarray(np.concatenate([c, -s], axis=1), BF16)
    return pl.pallas_call(
        functools.partial(_fourier_ctx_kernel, gd=gd),
        grid=(b,),
        in_specs=[
            pl.BlockSpec((None, l, d), lambda i: (i, 0, 0)),
            pl.BlockSpec((None, 3, d), lambda i: (0, 0, 0)),
            _const_spec((1, d)),
            _const_spec((1, d)),
            _const_spec(chan.shape),
            _const_spec(seqm.shape),
            _pick_spec(w_out, lead),
        ],
        out_specs=pl.BlockSpec((None, l, d), lambda i: (i, 0, 0)),
        out_shape=jax.ShapeDtypeStruct(x.shape, F32),
        compiler_params=_params("parallel"),
        name="fourier_ctx",
    )(x, mod, gpre, gpost, chan, seqm, w_out)


def _lower_bound(logits_ref, j):
    lg = logits_ref[...]
    e = jnp.exp(lg - jnp.max(lg, axis=0, keepdims=True))
    p = e / jnp.sum(e, axis=0, keepdims=True)
    lb = jnp.zeros_like(p[0:1])
    for i in range(1, j + 1):
        lb = lb + p[i:i + 1]
    return lb


def _forget_gate(z, lb):
    e = jnp.exp(-jnp.abs(z))
    inv = 1.0 / (1.0 + e)
    pos = z >= 0.0
    sig = jnp.where(pos, inv, e * inv)
    nsig = jnp.where(pos, e * inv, inv)
    f = jnp.maximum(lb, LB_FLOOR) + (1.0 - lb) * sig
    return (1.0 - lb) * nsig, jnp.log2(f)


def _split2(x):
    hi = x.astype(BF16)
    lo = (x - hi.astype(F32)).astype(BF16)
    return hi, lo


def _tile_cumsum(g, tri_ref, tile, reverse):
    c = tri_ref.shape[0]
    parts = _split2(g)
    chunks = [sum(_dot(tri_ref[...], p[i:i + c]) for p in parts) for i in range(0, g.shape[0], c)]
    per_tile = tile // c
    order = range(per_tile - 1, -1, -1) if reverse else range(per_tile)
    last = 0 if reverse else c - 1
    for t0 in range(0, len(chunks), per_tile):
        carry = None
        for i in order:
            if carry is not None:
                chunks[t0 + i] = chunks[t0 + i] + carry
            carry = chunks[t0 + i][last:last + 1, :]
    return jnp.concatenate(chunks, axis=0)


def _hgrn_in_kernel(x_ref, mod_ref, gpre_ref, w_ref, lbf_ref, lbb_ref, trif_ref, trib_ref,
                    q_ref, v_ref, kf_ref, bf_ref, kb_ref, bb_ref, sg_ref, *, layer_j, kd, hd, tile):
    hb = _pre(x_ref[...], mod_ref, gpre_ref).astype(BF16)
    col = lambda n: _dot(hb, w_ref[:, n * kd:(n + 1) * kd])
    q = _silu(col(0))
    v = col(1)
    kf, lf = _forget_gate(col(2), _lower_bound(lbf_ref, layer_j))
    kb, lbw = _forget_gate(col(3), _lower_bound(lbb_ref, layer_j))
    sg_ref[...] = _silu(col(4)).astype(BF16)
    bf = _tile_cumsum(lf, trif_ref, tile, False)
    bb = _tile_cumsum(lbw, trib_ref, tile, True)
    for h in range(HGRN_HEADS):
        sl = slice(h * hd, (h + 1) * hd)
        q_ref[h] = q[:, sl].astype(BF16)
        v_ref[h] = v[:, sl].astype(BF16)
        kf_ref[h] = kf[:, sl].astype(BF16)
        kb_ref[h] = kb[:, sl].astype(BF16)
        bf_ref[h] = bf[:, sl]
        bb_ref[h] = bb[:, sl]


def _hgrn_inputs(x, mod, gpre, w_in, lead, lb_fwd, lb_bwd, layer_j):
    b, l, d = x.shape
    kd = d
    hd = kd // HGRN_HEADS
    tm = _token_tile(l, 512)
    tile = _token_tile(tm, SCAN_TILE)
    t = _token_tile(tile, MXU_COLS)
    trif = _scan_consts(t, False)[0]
    trib = _scan_consts(t, True)[0]
    x_spec = pl.BlockSpec((None, tm, d), lambda i, j: (i, j, 0))
    head_spec = pl.BlockSpec((None, HGRN_HEADS, tm, hd), lambda i, j: (i, 0, j, 0))
    heads = lambda dt: jax.ShapeDtypeStruct((b, HGRN_HEADS, l, hd), dt)
    return pl.pallas_call(
        functools.partial(_hgrn_in_kernel, layer_j=layer_j, kd=kd, hd=hd, tile=tile),
        grid=(b, l // tm),
        in_specs=[
            x_spec,
            pl.BlockSpec((None, 3, d), lambda i, j: (i, 0, 0)),
            _const_spec((1, d)),
            _pick_spec(w_in, lead),
            _const_spec(lb_fwd.shape),
            _const_spec(lb_bwd.shape),
            _const_spec((t, t)),
            _const_spec((t, t)),
        ],
        out_specs=[head_spec] * 6 + [x_spec],
        out_shape=[heads(BF16), heads(BF16), heads(BF16), heads(F32), heads(BF16), heads(F32),
                   jax.ShapeDtypeStruct((b, l, d), BF16)],
        compiler_params=_params("parallel", "parallel"),
        name="hgrn_in",
    )(x, mod, gpre, w_in, lb_fwd, lb_bwd, trif, trib)


def _scan_consts(n, reverse):
    r = np.arange(n)[:, None]
    c = np.arange(n)[None, :]
    valid = (c >= r) if reverse else (c <= r)
    x = r ^ c
    lvl = np.where(x == 0, 0, np.floor(np.log2(np.maximum(x, 1))).astype(np.int64) + 1)
    fine = np.where(valid & (x < SUBLANES), lvl, -1)
    gap = np.where(valid, x, n)
    return jnp.asarray(valid, BF16), jnp.asarray(fine, jnp.int32), jnp.asarray(gap, jnp.int32)


def _level_ref(b, m, reverse):
    t, n = b.shape
    first = m if reverse else m - 1
    b3 = b.reshape(t // SUBLANES, SUBLANES, n)
    rid = lax.broadcasted_iota(jnp.int32, b3.shape, 1)
    ref = None
    for p in range(0, SUBLANES, 2 * m):
        row = jnp.broadcast_to(b3[:, p + first:p + first + 1, :], b3.shape)
        ref = row if ref is None else jnp.where(rid >= p, row, ref)
    return ref.reshape(t, n)


def _neg_abs(d):
    bits = lax.bitcast_convert_type(d, jnp.int32) | jnp.int32(-2 ** 31)
    return lax.bitcast_convert_type(bits, F32)


def _scaled(z, e):
    return (z * e).astype(BF16)


def _halves(lo, m, reverse):
    return (lo, lo + m, lo + m) if reverse else (lo + m, lo, lo + m - 1)


def _strip_scores(q, k, q32, k32, b, fine, gap, lane, reverse, fast):
    n = q.shape[0]
    if fast:
        first = fast - 1 if reverse else 0
        ref = jnp.concatenate([jnp.broadcast_to(b[i + first:i + first + 1, :], (fast, b.shape[1]))
                               for i in range(0, n, fast)], axis=0)
        d = b - ref
        a = jnp.where(gap < fast, _dot_nt(_scaled(q32, jnp.exp2(d)), _scaled(k32, jnp.exp2(-d))), 0.0)
        m = fast
    else:
        a = jnp.where(fine == 0, _dot_nt(q, k), 0.0)
        before = pltpu.roll(b, (n - 1) if reverse else 1, 0)
        a = jnp.where(fine == 1, _dot_nt(_scaled(q32, jnp.exp2(_neg_abs(b - before))), k), a)
        m, level = 2, 2
        while m < SUBLANES:
            e = jnp.exp2(_neg_abs(b - _level_ref(b, m, reverse)))
            a = jnp.where(fine == level, _dot_nt(_scaled(q32, e), _scaled(k32, e)), a)
            m, level = 2 * m, level + 1
    blocks = [a[i:i + SUBLANES] for i in range(0, n, SUBLANES)]
    while m < n:
        ql, kf = [], []
        for lo in range(0, n, 2 * m):
            late, early, first = _halves(lo, m, reverse)
            ref = b[first:first + 1, :]
            ql.append(_scaled(q32[late:late + m], jnp.exp2(b[late:late + m] - ref)))
            ke = _scaled(k32[early:early + m], jnp.exp2(ref - b[early:early + m]))
            kf += [k[late:late + m], ke] if reverse else [ke, k[late:late + m]]
        s_m = _dot_nt(jnp.concatenate(ql, axis=0), jnp.concatenate(kf, axis=0))
        for pi, lo in enumerate(range(0, n, 2 * m)):
            late, early, _ = _halves(lo, m, reverse)
            inside = (lane >= early) & (lane < early + m)
            for i in range(0, m, SUBLANES):
                rb = (late + i) // SUBLANES
                blocks[rb] = jnp.where(inside, s_m[pi * m + i:pi * m + i + SUBLANES], blocks[rb])
        m *= 2
    return jnp.concatenate(blocks, axis=0)


def _scan_tile(q, k, v, b, st, fine, gap, reverse, fast):
    t = q.shape[0]
    n = fine.shape[0]
    q32 = q.astype(F32)
    k32 = k.astype(F32)
    end = 0 if reverse else t - 1
    b_end = b[end:end + 1, :]

    o = _dot_nt(_scaled(q32, jnp.exp2(b)), st.astype(BF16))
    new_st = st * jnp.exp2(b_end) + _dot_tn(v, _scaled(k32, jnp.exp2(b_end - b)))

    lane = lax.broadcasted_iota(jnp.int32, (SUBLANES, n), 1)
    strips = range(0, t, n)
    score = {(i, i): _strip_scores(q[i:i + n], k[i:i + n], q32[i:i + n], k32[i:i + n], b[i:i + n],
                                   fine, gap, lane, reverse, fast) for i in strips}
    m = n
    while m < t:
        for lo in range(0, t, 2 * m):
            late, early, first = _halves(lo, m, reverse)
            ref = b[first:first + 1, :]
            ql = _scaled(q32[late:late + m], jnp.exp2(b[late:late + m] - ref))
            ke = _scaled(k32[early:early + m], jnp.exp2(ref - b[early:early + m]))
            s_m = _dot_nt(ql, ke)
            for i in range(0, m, n):
                for j in range(0, m, n):
                    score[(late + i, early + j)] = s_m[i:i + n, j:j + n]
        m *= 2
    outs = []
    for i in strips:
        keys = [j for j in strips if (i, j) in score]
        a = jnp.concatenate([score[(i, j)] for j in keys], axis=1).astype(BF16)
        outs.append(_dot(a, jnp.concatenate([v[j:j + n] for j in keys], axis=0)))
    return o + jnp.concatenate(outs, axis=0), new_st


def _scan_kernel(q_ref, k_ref, v_ref, b_ref, s0_ref, fine_ref, gap_ref, o_ref, sfin_ref, state_ref,
                 *, reverse, n_tiles):
    step = pl.program_id(2)

    @pl.when(step == 0)
    def _():
        state_ref[...] = s0_ref[...]

    hp, t, _ = q_ref.shape

    def spread(blk):
        worst = None
        for h in range(hp):
            top = b_ref[h, pl.ds(0, t // blk, stride=blk), :]
            bottom = b_ref[h, pl.ds(blk - 1, t // blk, stride=blk), :]
            s = jnp.abs(top - bottom)
            worst = s if worst is None else jnp.maximum(worst, s)
        return jnp.max(worst)

    def run(fast):
        fine = fine_ref[...]
        gap = gap_ref[...]
        for h in range(hp):
            o, st = _scan_tile(q_ref[h], k_ref[h], v_ref[h], b_ref[h], state_ref[h], fine, gap, reverse, fast)
            state_ref[h] = st
            o_ref[h] = o.astype(o_ref.dtype)

    blocks = [blk for blk in FAST_BLOCKS if t % blk == 0]
    fits = [spread(blk) <= FAST_RANGE for blk in blocks]
    unresolved = None
    for blk, ok in zip(blocks, fits):
        pl.when(ok if unresolved is None else unresolved & ok)(functools.partial(run, blk))
        unresolved = jnp.logical_not(ok) if unresolved is None else unresolved & jnp.logical_not(ok)
    if unresolved is None:
        run(0)
    else:
        pl.when(unresolved)(functools.partial(run, 0))

    @pl.when(step == n_tiles - 1)
    def _():
        sfin_ref[...] = state_ref[...]


def _scan(q, k, v, bcum, s0, reverse):
    b, h, l, hd = q.shape
    t = _token_tile(l, SCAN_TILE)
    n_tiles = l // t
    hp = SCAN_HEADS
    _, fine, gap = _scan_consts(LANES, reverse)
    tile = (lambda j: n_tiles - 1 - j) if reverse else (lambda j: j)
    tok_spec = pl.BlockSpec((None, hp, t, hd), lambda i, hh, j: (i, hh, tile(j), 0))
    st_spec = pl.BlockSpec((None, hp, hd, hd), lambda i, hh, j: (i, hh, 0, 0))
    return pl.pallas_call(
        functools.partial(_scan_kernel, reverse=reverse, n_tiles=n_tiles),
        grid=(b, h // hp, n_tiles),
        in_specs=[tok_spec, tok_spec, tok_spec, tok_spec, st_spec, _const_spec(fine.shape), _const_spec(gap.shape)],
        out_specs=[tok_spec, st_spec],
        out_shape=[jax.ShapeDtypeStruct((b, h, l, hd), BF16), jax.ShapeDtypeStruct((b, h, hd, hd), F32)],
        scratch_shapes=[pltpu.VMEM((hp, hd, hd), F32)],
        compiler_params=_params("parallel", "parallel", "arbitrary"),
        name="scan_bwd" if reverse else "scan_fwd",
    )(q, k, v, bcum, s0, fine, gap)


def _hgrn_out_ffn_kernel(of_ref, ob_ref, sg_ref, x_ref, mod_ref, gpost_ref, gn_ref, w_ref,
                         fmod_ref, fgpre_ref, fgpost_ref, win_ref, wout_ref, o_ref):
    heads = []
    for h in range(HGRN_HEADS):
        heads.append(_rms(of_ref[h].astype(F32) + ob_ref[h].astype(F32), gn_ref[...]))
    o = jnp.concatenate(heads, axis=1)
    y = _dot((o * sg_ref[...].astype(F32)).astype(BF16), w_ref[...])
    x = _post(x_ref[...], y, mod_ref, gpost_ref, 1.0)
    o_ref[...] = _ffn_step(x, fmod_ref, fgpre_ref, fgpost_ref, win_ref, wout_ref)


def _hgrn_readout_ffn(o_f, o_b, sg, x, mod, gpost, g_norm, w_out, lead, fmod, fgpre, fgpost, w_in, w_ffn, flead):
    b, l, d = x.shape
    hd = d // HGRN_HEADS
    assert w_ffn.shape[-2] % MXU_COLS == 0
    tm = _token_tile(l, 512)
    x_spec = pl.BlockSpec((None, tm, d), lambda i, j: (i, j, 0))
    head_spec = pl.BlockSpec((None, HGRN_HEADS, tm, hd), lambda i, j: (i, 0, j, 0))
    mod_spec = pl.BlockSpec((None, 3, d), lambda i, j: (i, 0, 0))
    return pl.pallas_call(
        _hgrn_out_ffn_kernel,
        grid=(b, l // tm),
        in_specs=[
            head_spec, head_spec, x_spec, x_spec,
            mod_spec,
            _const_spec((1, d)),
            _const_spec((1, hd)),
            _pick_spec(w_out, lead),
            mod_spec,
            _const_spec((1, d)),
            _const_spec((1, d)),
            _pick_spec(w_in, flead),
            _pick_spec(w_ffn, flead),
        ],
        out_specs=x_spec,
        out_shape=jax.ShapeDtypeStruct(x.shape, F32),
        compiler_params=_params("parallel", "parallel"),
        name="hgrn_out_ffn",
    )(o_f, o_b, sg, x, mod, gpost, g_norm, w_out, fmod, fgpre, fgpost, w_in, w_ffn)


def kernel(x, c, ctx, c_ctx, ada_w, ada_b, norm_pre, norm_post, ffn_w_in, ffn_w_out, fourier_w_out,
           hgrn_w_in, hgrn_lb_fwd, hgrn_lb_bwd, hgrn_norm, hgrn_w_out):
    bsz, _, d = x.shape
    depth = ada_w.shape[0]
    assert bsz + 1 <= MOD_ROWS

    c_rows = jnp.concatenate([c, c_ctx[None, :], jnp.zeros((MOD_ROWS - bsz - 1, d), F32)], axis=0)
    mod = _adaln(c_rows, ada_w, ada_b).reshape(depth, MOD_ROWS, N_MOD, d)

    ffn_w_in = ffn_w_in.astype(BF16)
    ffn_w_out = ffn_w_out.astype(BF16)
    fourier_w_out = fourier_w_out.astype(BF16)
    hgrn_w_in = hgrn_w_in.astype(BF16)
    hgrn_w_out = hgrn_w_out.astype(BF16)

    for i in range(depth):
        last = i == depth - 1
        is_hgrn = i % 2 == 1
        jm = i // 2
        mx = lambda j: mod[i, :bsz, 3 * j:3 * j + 3]
        mc = lambda j: mod[i, bsz:bsz + 1, 3 * j:3 * j + 3]
        gpre = lambda j: norm_pre[i, j][None, :]
        gpost = lambda j: norm_post[i, j][None, :]
        ctx_mod = lambda j: jnp.broadcast_to(mc(j), (bsz, 3, d))
        ffn = lambda s, m, j, f: _ffn(s, m(j), gpre(j), gpost(j), ffn_w_in, ffn_w_out, (i, f))

        def ctx_ffn(s, j, f):
            n = s.shape[0] * s.shape[1]
            rows = FFN_TOKENS if n % FFN_TOKENS == 0 else s.shape[1]
            m = jnp.broadcast_to(mc(j), (n // rows, 3, d))
            return _ffn(s.reshape(n // rows, rows, d), m, gpre(j), gpost(j), ffn_w_in, ffn_w_out, (i, f)).reshape(s.shape)
        ctx_first = is_hgrn or not last

        if not is_hgrn:
            x = _ffn_fourier_latent(x, (mx(0), mx(2)), (gpre(0), gpre(2)), (gpost(0), gpost(2)),
                                    ffn_w_in, ffn_w_out, (i, 0), (i, 1),
                                    mx(1), gpre(1), gpost(1), fourier_w_out, (jm,))
            if ctx_first:
                ctx = ctx_ffn(ctx, 0, 0)
            if not last:
                ctx = _fourier_ctx(ctx, mc(1), gpre(1), gpost(1), fourier_w_out, (jm,))
                ctx = ctx_ffn(ctx, 2, 1)
        else:
            x = ffn(x, mx, 0, 0)
            ctx = ctx_ffn(ctx, 0, 0)
            hd = d // HGRN_HEADS
            zero = jnp.zeros((bsz, HGRN_HEADS, hd, hd), F32)
            qc, vc, kfc, bfc, kbc, bbc, sgc = _hgrn_inputs(
                ctx, ctx_mod(1), gpre(1), hgrn_w_in, (jm,), hgrn_lb_fwd, hgrn_lb_bwd, jm)
            ocf, s_f = _scan(qc, kfc, vc, bfc, zero, False)
            ocb, s_b = _scan(qc, kbc, vc, bbc, zero, True)
            qx, vx, kfx, bfx, kbx, bbx, sgx = _hgrn_inputs(
                x, mx(1), gpre(1), hgrn_w_in, (jm,), hgrn_lb_fwd, hgrn_lb_bwd, jm)
            oxf, _ = _scan(qx, kfx, vx, bfx, s_f, False)
            oxb, _ = _scan(qx, kbx, vx, bbx, s_b, True)
            gn = hgrn_norm[jm][None, :]
            x = _hgrn_readout_ffn(oxf, oxb, sgx, x, mx(1), gpost(1), gn, hgrn_w_out, (jm,),
                                  mx(2), gpre(2), gpost(2), ffn_w_in, ffn_w_out, (i, 1))
            if not last:
                ctx = _hgrn_readout_ffn(ocf, ocb, sgc, ctx, ctx_mod(1), gpost(1), gn, hgrn_w_out, (jm,),
                                        ctx_mod(2), gpre(2), gpost(2), ffn_w_in, ffn_w_out, (i, 1))
    return x
```

```python
import functools

import jax
import jax.numpy as jnp
import numpy as np
from jax import lax
from jax.experimental import pallas as pl
from jax.experimental.pallas import tpu as pltpu

F32 = jnp.float32
BF16 = jnp.bfloat16

GRID_W = 64
FOURIER_GROUPS = 4
HGRN_HEADS = 8
N_MOD = 9
HALF = 0.5
NORM_EPS = 1e-6
LB_FLOOR = 1e-30

VMEM_LIMIT_V7X = 56 * 1024 * 1024
SUBLANES = 8
LANES = 128
MXU_COLS = 256
FFN_TOKENS = 1024
SCAN_TILE = 512
FAST_BLOCKS = (128, 64, 32, 16)
FAST_RANGE = 200.0
SCAN_HEADS = 8
MOD_ROWS = 16


def _params(*sem):
    return pltpu.CompilerParams(dimension_semantics=sem, vmem_limit_bytes=VMEM_LIMIT_V7X)


def _const_spec(shape):
    n = len(shape)
    return pl.BlockSpec(shape, lambda *_: (0,) * n, pipeline_mode=pl.Buffered(1))


def _pick_spec(stacked, lead):
    tail = stacked.shape[len(lead):]
    return pl.BlockSpec((None,) * len(lead) + tail, lambda *_: tuple(lead) + (0,) * len(tail),
                        pipeline_mode=pl.Buffered(1))


def _rms(x, g):
    ms = jnp.mean(x * x, axis=-1, keepdims=True)
    return x * lax.rsqrt(ms + NORM_EPS) * g


def _pre(x, mod_ref, g_ref):
    return _rms(x, g_ref[...]) * (1.0 + mod_ref[1:2, :]) + mod_ref[0:1, :]


def _post(x, y, mod_ref, g_ref, w):
    return x + w * mod_ref[2:3, :] * _rms(y, g_ref[...])


def _silu(x):
    return x * jax.nn.sigmoid(x)


def _dot(a, b):
    return jnp.dot(a, b, preferred_element_type=F32)


def _dot_nt(a, b):
    return lax.dot_general(a, b, (((1,), (1,)), ((), ())), preferred_element_type=F32)


def _dot_tn(a, b):
    return lax.dot_general(a, b, (((0,), (0,)), ((), ())), preferred_element_type=F32)


def _adaln_kernel(c_ref, w_ref, b_ref, o_ref):
    sc = _silu(c_ref[...]).astype(BF16)
    o_ref[...] = _dot(sc, w_ref[...].astype(BF16)) + b_ref[...]


def _adaln(c_rows, ada_w, ada_b):
    depth, d, n = ada_w.shape
    tn = 1024
    return pl.pallas_call(
        _adaln_kernel,
        grid=(depth, n // tn),
        in_specs=[
            pl.BlockSpec((MOD_ROWS, d), lambda i, j: (0, 0)),
            pl.BlockSpec((None, d, tn), lambda i, j: (i, 0, j)),
            pl.BlockSpec((None, 1, tn), lambda i, j: (i, 0, j)),
        ],
        out_specs=pl.BlockSpec((None, MOD_ROWS, tn), lambda i, j: (i, 0, j)),
        out_shape=jax.ShapeDtypeStruct((depth, MOD_ROWS, n), F32),
        compiler_params=_params("parallel", "parallel"),
        name="adaln",
    )(c_rows, ada_w, ada_b.reshape(depth, 1, n))


def _ffn_step(x, mod_ref, gpre_ref, gpost_ref, win_ref, wout_ref):
    d_ff = wout_ref.shape[0]
    hb = _pre(x, mod_ref, gpre_ref).astype(BF16)
    fc = MXU_COLS
    acc = None
    for ci in range(d_ff // fc):
        gate = _dot(hb, win_ref[:, ci * fc:(ci + 1) * fc])
        up = _dot(hb, win_ref[:, d_ff + ci * fc:d_ff + (ci + 1) * fc])
        a = (_silu(gate) * up).astype(BF16)
        y = _dot(a, wout_ref[ci * fc:(ci + 1) * fc, :])
        acc = y if acc is None else acc + y
    return _post(x, acc, mod_ref, gpost_ref, HALF)


def _ffn_kernel(x_ref, mod_ref, gpre_ref, gpost_ref, win_ref, wout_ref, o_ref):
    o_ref[...] = _ffn_step(x_ref[...], mod_ref, gpre_ref, gpost_ref, win_ref, wout_ref)


def _token_tile(n_tokens, want):
    return want if n_tokens % want == 0 else n_tokens


def _ffn(s, mod, gpre, gpost, w_in, w_out, lead):
    b, l, d = s.shape
    d_ff = w_out.shape[-2]
    assert d_ff % MXU_COLS == 0
    tm = _token_tile(l, FFN_TOKENS)
    return pl.pallas_call(
        _ffn_kernel,
        grid=(b, l // tm),
        in_specs=[
            pl.BlockSpec((None, tm, d), lambda i, j: (i, j, 0)),
            pl.BlockSpec((None, 3, d), lambda i, j: (i, 0, 0)),
            _const_spec((1, d)),
            _const_spec((1, d)),
            _pick_spec(w_in, lead),
            _pick_spec(w_out, lead),
        ],
        out_specs=pl.BlockSpec((None, tm, d), lambda i, j: (i, j, 0)),
        out_shape=jax.ShapeDtypeStruct(s.shape, F32),
        compiler_params=_params("parallel", "parallel"),
        name="ffn",
    )(s, mod, gpre, gpost, w_in, w_out)


def _dft_cos_sin(n):
    k = np.arange(n)
    ang = 2.0 * np.pi * ((k[:, None] * k[None, :]) % n) / n
    s = 1.0 / np.sqrt(n)
    return np.cos(ang) * s, np.sin(ang) * s


def _fourier_consts(gd):
    c, s = _dft_cos_sin(gd)
    chan = np.concatenate([c, s], axis=1)
    c, s = _dft_cos_sin(GRID_W)
    col = np.block([[c, -s], [s, c]])
    return jnp.asarray(chan, BF16), jnp.asarray(col, BF16)


def _channel_dft(hb, chan_ref, gd):
    us, vs = [], []
    for g in range(FOURIER_GROUPS):
        uv = _dot(hb[:, g * gd:(g + 1) * gd], chan_ref[...])
        us.append(uv[:, :gd])
        vs.append(uv[:, gd:])
    return (jnp.concatenate(us, axis=1).astype(BF16), jnp.concatenate(vs, axis=1).astype(BF16))


def _pack_pair(a, b):
    hi = lax.bitcast_convert_type(a.astype(BF16).astype(F32), jnp.uint32)
    lo = lax.bitcast_convert_type(b.astype(BF16).astype(F32), jnp.uint32)
    return hi | (lo >> 16)


def _unpack_pair(p):
    a = lax.bitcast_convert_type(p & jnp.uint32(0xFFFF0000), F32)
    b = lax.bitcast_convert_type(p << 16, F32)
    return a.astype(BF16), b.astype(BF16)


def _ffn_fourier_cols_kernel(x_ref, fmod_ref, fgpre_ref, fgpost_ref, win_ref, wout_ref,
                             mod_ref, gpre_ref, chan_ref, col_ref, o_ref, ab_ref, *, gd):
    x = _ffn_step(x_ref[...], fmod_ref, fgpre_ref, fgpost_ref, win_ref, wout_ref)
    o_ref[...] = x
    hb = _pre(x, mod_ref, gpre_ref).astype(BF16)
    u, v = _channel_dft(hb, chan_ref, gd)
    for r in range(x.shape[0] // GRID_W):
        sl = slice(r * GRID_W, (r + 1) * GRID_W)
        ab = _dot(col_ref[...], jnp.concatenate([u[sl], v[sl]], axis=0))
        ab_ref[sl, :] = _pack_pair(ab[:GRID_W], ab[GRID_W:])


def _fourier_rows_ffn_kernel(ab_ref, x_ref, mod_ref, gpost_ref, rowk_ref, w_ref,
                             fmod_ref, fgpre_ref, fgpost_ref, win_ref, wout_ref, o_ref):
    rows, cps, d = x_ref.shape
    flat = lambda v: v.reshape(rows * cps, d)
    a, b = _unpack_pair(flat(ab_ref[...]))
    y = _dot(rowk_ref[...], jnp.concatenate([a, b], axis=0)).astype(BF16)
    x = _post(flat(x_ref[...]), _dot(y, w_ref[...]), mod_ref, gpost_ref, 1.0)
    o_ref[...] = _ffn_step(x, fmod_ref, fgpre_ref, fgpost_ref, win_ref, wout_ref).reshape(rows, cps, d)


def _ffn_fourier_latent(x, fmod, fgpre, fgpost, w_in, w_ffn, lead_a, lead_b, mod, gpre, gpost, w_out, lead):
    b, l, d = x.shape
    rows = l // GRID_W
    gd = d // FOURIER_GROUPS
    assert w_ffn.shape[-2] % MXU_COLS == 0
    chan, col = _fourier_consts(gd)
    tm = _token_tile(l, 512)
    assert tm % GRID_W == 0
    tok = pl.BlockSpec((None, tm, d), lambda i, j: (i, j, 0))
    mod_spec = pl.BlockSpec((None, 3, d), lambda i, j: (i, 0, 0))
    x, ab = pl.pallas_call(
        functools.partial(_ffn_fourier_cols_kernel, gd=gd),
        grid=(b, l // tm),
        in_specs=[
            tok, mod_spec, _const_spec((1, d)), _const_spec((1, d)),
            _pick_spec(w_in, lead_a), _pick_spec(w_ffn, lead_a),
            mod_spec, _const_spec((1, d)), _const_spec(chan.shape), _const_spec(col.shape),
        ],
        out_specs=[tok, tok],
        out_shape=[jax.ShapeDtypeStruct((b, l, d), F32), jax.ShapeDtypeStruct((b, l, d), jnp.uint32)],
        compiler_params=_params("parallel", "parallel"),
        name="ffn_fourier_cols",
    )(x, fmod[0], fgpre[0], fgpost[0], w_in, w_ffn, mod, gpre, chan, col)

    c, s = _dft_cos_sin(rows)
    cps = SUBLANES
    eye = np.eye(cps)
    rowk = jnp.asarray(np.concatenate([np.kron(c, eye), -np.kron(s, eye)], axis=1), BF16)
    grid_view = lambda t: t.reshape(b, rows, GRID_W, d)
    blk = pl.BlockSpec((None, rows, cps, d), lambda i, j: (i, 0, j, 0))
    out = pl.pallas_call(
        _fourier_rows_ffn_kernel,
        grid=(b, GRID_W // cps),
        in_specs=[
            blk, blk, mod_spec, _const_spec((1, d)), _const_spec(rowk.shape), _pick_spec(w_out, lead),
            mod_spec, _const_spec((1, d)), _const_spec((1, d)),
            _pick_spec(w_in, lead_b), _pick_spec(w_ffn, lead_b),
        ],
        out_specs=blk,
        out_shape=jax.ShapeDtypeStruct((b, rows, GRID_W, d), F32),
        compiler_params=_params("parallel", "parallel"),
        name="fourier_rows_ffn",
    )(grid_view(ab), grid_view(x), mod, gpost, rowk, w_out, fmod[1], fgpre[1], fgpost[1], w_in, w_ffn)
    return out.reshape(b, l, d)


def _fourier_ctx_kernel(x_ref, mod_ref, gpre_ref, gpost_ref, chan_ref, seq_ref, w_ref, o_ref, *, gd):
    x = x_ref[...]
    hb = _pre(x, mod_ref, gpre_ref).astype(BF16)
    u, v = _channel_dft(hb, chan_ref, gd)
    y = _dot(seq_ref[...], jnp.concatenate([u, v], axis=0)).astype(BF16)
    o_ref[...] = _post(x, _dot(y, w_ref[...]), mod_ref, gpost_ref, 1.0)


def _fourier_ctx(x, mod, gpre, gpost, w_out, lead):
    b, l, d = x.shape
    gd = d // FOURIER_GROUPS
    chan, _ = _fourier_consts(gd)
    c, s = _dft_cos_sin(l)
    seqm = jnp.asarray(np.concatenate([c, -s], axis=1), BF16)
    return pl.pallas_call(
        functools.partial(_fourier_ctx_kernel, gd=gd),
        grid=(b,),
        in_specs=[
            pl.BlockSpec((None, l, d), lambda i: (i, 0, 0)),
            pl.BlockSpec((None, 3, d), lambda i: (0, 0, 0)),
            _const_spec((1, d)),
            _const_spec((1, d)),
            _const_spec(chan.shape),
            _const_spec(seqm.shape),
            _pick_spec(w_out, lead),
        ],
        out_specs=pl.BlockSpec((None, l, d), lambda i: (i, 0, 0)),
        out_shape=jax.ShapeDtypeStruct(x.shape, F32),
        compiler_params=_params("parallel"),
        name="fourier_ctx",
    )(x, mod, gpre, gpost, chan, seqm, w_out)


def _lower_bound(logits_ref, j):
    lg = logits_ref[...]
    e = jnp.exp(lg - jnp.max(lg, axis=0, keepdims=True))
    p = e / jnp.sum(e, axis=0, keepdims=True)
    lb = jnp.zeros_like(p[0:1])
    for i in range(1, j + 1):
        lb = lb + p[i:i + 1]
    return lb


def _forget_gate(z, lb):
    e = jnp.exp(-jnp.abs(z))
    inv = 1.0 / (1.0 + e)
    pos = z >= 0.0
    sig = jnp.where(pos, inv, e * inv)
    nsig = jnp.where(pos, e * inv, inv)
    f = jnp.maximum(lb, LB_FLOOR) + (1.0 - lb) * sig
    return (1.0 - lb) * nsig, jnp.log2(f)


def _split2(x):
    hi = x.astype(BF16)
    lo = (x - hi.astype(F32)).astype(BF16)
    return hi, lo


def _tile_cumsum(g, tri_ref, tile, reverse):
    c = tri_ref.shape[0]
    parts = _split2(g)
    tri2 = jnp.concatenate([tri_ref[...]] * len(parts), axis=1)
    chunks = [_dot(tri2, jnp.concatenate([p[i:i + c] for p in parts], axis=0)) for i in range(0, g.shape[0], c)]
    per_tile = tile // c
    order = range(per_tile - 1, -1, -1) if reverse else range(per_tile)
    last = 0 if reverse else c - 1
    for t0 in range(0, len(chunks), per_tile):
        carry = None
        for i in order:
            if carry is not None:
                chunks[t0 + i] = chunks[t0 + i] + carry
            carry = chunks[t0 + i][last:last + 1, :]
    return jnp.concatenate(chunks, axis=0)


def _hgrn_in_kernel(x_ref, mod_ref, gpre_ref, w_ref, lbf_ref, lbb_ref, trif_ref, trib_ref,
                    q_ref, v_ref, kf_ref, bf_ref, kb_ref, bb_ref, sg_ref, *, layer_j, kd, hd, tile):
    hb = _pre(x_ref[...], mod_ref, gpre_ref).astype(BF16)
    col = lambda n: _dot(hb, w_ref[:, n * kd:(n + 1) * kd])
    q = _silu(col(0))
    v = col(1)
    kf, lf = _forget_gate(col(2), _lower_bound(lbf_ref, layer_j))
    kb, lbw = _forget_gate(col(3), _lower_bound(lbb_ref, layer_j))
    sg_ref[...] = _silu(col(4)).astype(BF16)
    bf = _tile_cumsum(lf, trif_ref, tile, False)
    bb = _tile_cumsum(lbw, trib_ref, tile, True)
    for h in range(HGRN_HEADS):
        sl = slice(h * hd, (h + 1) * hd)
        q_ref[h] = q[:, sl].astype(BF16)
        v_ref[h] = v[:, sl].astype(BF16)
        kf_ref[h] = kf[:, sl].astype(BF16)
        kb_ref[h] = kb[:, sl].astype(BF16)
        bf_ref[h] = bf[:, sl]
        bb_ref[h] = bb[:, sl]


def _hgrn_inputs(x, mod, gpre, w_in, lead, lb_fwd, lb_bwd, layer_j):
    b, l, d = x.shape
    kd = d
    hd = kd // HGRN_HEADS
    tm = _token_tile(l, 512)
    tile = _token_tile(tm, SCAN_TILE)
    t = _token_tile(tile, MXU_COLS)
    trif = _scan_consts(t, False)[0]
    trib = _scan_consts(t, True)[0]
    x_spec = pl.BlockSpec((None, tm, d), lambda i, j: (i, j, 0))
    head_spec = pl.BlockSpec((None, HGRN_HEADS, tm, hd), lambda i, j: (i, 0, j, 0))
    heads = lambda dt: jax.ShapeDtypeStruct((b, HGRN_HEADS, l, hd), dt)
    return pl.pallas_call(
        functools.partial(_hgrn_in_kernel, layer_j=layer_j, kd=kd, hd=hd, tile=tile),
        grid=(b, l // tm),
        in_specs=[
            x_spec,
            pl.BlockSpec((None, 3, d), lambda i, j: (i, 0, 0)),
            _const_spec((1, d)),
            _pick_spec(w_in, lead),
            _const_spec(lb_fwd.shape),
            _const_spec(lb_bwd.shape),
            _const_spec((t, t)),
            _const_spec((t, t)),
        ],
        out_specs=[head_spec] * 6 + [x_spec],
        out_shape=[heads(BF16), heads(BF16), heads(BF16), heads(F32), heads(BF16), heads(F32),
                   jax.ShapeDtypeStruct((b, l, d), BF16)],
        compiler_params=_params("parallel", "parallel"),
        name="hgrn_in",
    )(x, mod, gpre, w_in, lb_fwd, lb_bwd, trif, trib)


def _scan_consts(n, reverse):
    r = np.arange(n)[:, None]
    c = np.arange(n)[None, :]
    valid = (c >= r) if reverse else (c <= r)
    x = r ^ c
    lvl = np.where(x == 0, 0, np.floor(np.log2(np.maximum(x, 1))).astype(np.int64) + 1)
    fine = np.where(valid & (x < SUBLANES), lvl, -1)
    gap = np.where(valid, x, n)
    return jnp.asarray(valid, BF16), jnp.asarray(fine, jnp.int32), jnp.asarray(gap, jnp.int32)


def _level_ref(b, m, reverse):
    t, n = b.shape
    first = m if reverse else m - 1
    b3 = b.reshape(t // SUBLANES, SUBLANES, n)
    rid = lax.broadcasted_iota(jnp.int32, b3.shape, 1)
    ref = None
    for p in range(0, SUBLANES, 2 * m):
        row = jnp.broadcast_to(b3[:, p + first:p + first + 1, :], b3.shape)
        ref = row if ref is None else jnp.where(rid >= p, row, ref)
    return ref.reshape(t, n)


def _neg_abs(d):
    bits = lax.bitcast_convert_type(d, jnp.int32) | jnp.int32(-2 ** 31)
    return lax.bitcast_convert_type(bits, F32)


def _scaled(z, e):
    return (z * e).astype(BF16)


def _halves(lo, m, reverse):
    return (lo, lo + m, lo + m) if reverse else (lo + m, lo, lo + m - 1)


def _strip_scores(q, k, q32, k32, b, fine, gap, lane, reverse, fast):
    n = q.shape[0]
    if fast:
        mid = lambda i: 0.5 * (b[i:i + 1, :] + b[i + fast - 1:i + fast, :])
        ref = jnp.concatenate([jnp.broadcast_to(mid(i), (fast, b.shape[1])) for i in range(0, n, fast)], axis=0)
        d = b - ref
        a = jnp.where(gap < fast, _dot_nt(_scaled(q32, jnp.exp2(d)), _scaled(k32, jnp.exp2(-d))), 0.0)
        m = fast
    else:
        a = jnp.where(fine == 0, _dot_nt(q, k), 0.0)
        before = pltpu.roll(b, (n - 1) if reverse else 1, 0)
        a = jnp.where(fine == 1, _dot_nt(_scaled(q32, jnp.exp2(_neg_abs(b - before))), k), a)
        m, level = 2, 2
        while m < SUBLANES:
            e = jnp.exp2(_neg_abs(b - _level_ref(b, m, reverse)))
            a = jnp.where(fine == level, _dot_nt(_scaled(q32, e), _scaled(k32, e)), a)
            m, level = 2 * m, level + 1
    blocks = [a[i:i + SUBLANES] for i in range(0, n, SUBLANES)]
    while m < n:
        ql, kf = [], []
        for lo in range(0, n, 2 * m):
            late, early, first = _halves(lo, m, reverse)
            ref = b[first:first + 1, :]
            ql.append(_scaled(q32[late:late + m], jnp.exp2(b[late:late + m] - ref)))
            ke = _scaled(k32[early:early + m], jnp.exp2(ref - b[early:early + m]))
            kf += [k[late:late + m], ke] if reverse else [ke, k[late:late + m]]
        s_m = _dot_nt(jnp.concatenate(ql, axis=0), jnp.concatenate(kf, axis=0))
        for pi, lo in enumerate(range(0, n, 2 * m)):
            late, early, _ = _halves(lo, m, reverse)
            inside = (lane >= early) & (lane < early + m)
            for i in range(0, m, SUBLANES):
                rb = (late + i) // SUBLANES
                blocks[rb] = jnp.where(inside, s_m[pi * m + i:pi * m + i + SUBLANES], blocks[rb])
        m *= 2
    return jnp.concatenate(blocks, axis=0)


def _scan_tile(q, k, v, b, st, fine, gap, reverse, fast):
    t = q.shape[0]
    n = fine.shape[0]
    q32 = q.astype(F32)
    k32 = k.astype(F32)
    end = 0 if reverse else t - 1
    b_end = b[end:end + 1, :]

    o = _dot_nt(_scaled(q32, jnp.exp2(b)), st.astype(BF16))
    new_st = st * jnp.exp2(b_end) + _dot_tn(v, _scaled(k32, jnp.exp2(b_end - b)))

    lane = lax.broadcasted_iota(jnp.int32, (SUBLANES, n), 1)
    strips = range(0, t, n)
    score = {(i, i): _strip_scores(q[i:i + n], k[i:i + n], q32[i:i + n], k32[i:i + n], b[i:i + n],
                                   fine, gap, lane, reverse, fast) for i in strips}
    m = n
    while m < t:
        for lo in range(0, t, 2 * m):
            late, early, first = _halves(lo, m, reverse)
            ref = b[first:first + 1, :]
            ql = _scaled(q32[late:late + m], jnp.exp2(b[late:late + m] - ref))
            ke = _scaled(k32[early:early + m], jnp.exp2(ref - b[early:early + m]))
            s_m = _dot_nt(ql, ke)
            for i in range(0, m, n):
                for j in range(0, m, n):
                    score[(late + i, early + j)] = s_m[i:i + n, j:j + n]
        m *= 2
    outs = []
    for i in strips:
        keys = [j for j in strips if (i, j) in score]
        a = jnp.concatenate([score[(i, j)] for j in keys], axis=1).astype(BF16)
        outs.append(_dot(a, jnp.concatenate([v[j:j + n] for j in keys], axis=0)))
    return o + jnp.concatenate(outs, axis=0), new_st


def _scan_kernel(q_ref, k_ref, v_ref, b_ref, s0_ref, fine_ref, gap_ref, o_ref, sfin_ref, state_ref,
                 *, reverse, n_tiles):
    step = pl.program_id(2)

    @pl.when(step == 0)
    def _():
        state_ref[...] = s0_ref[...]

    hp, t, _ = q_ref.shape

    def spread(blk):
        worst = None
        for h in range(hp):
            top = b_ref[h, pl.ds(0, t // blk, stride=blk), :]
            bottom = b_ref[h, pl.ds(blk - 1, t // blk, stride=blk), :]
            s = jnp.abs(top - bottom)
            worst = s if worst is None else jnp.maximum(worst, s)
        return jnp.max(worst)

    def run(fast):
        fine = fine_ref[...]
        gap = gap_ref[...]
        for h in range(hp):
            o, st = _scan_tile(q_ref[h], k_ref[h], v_ref[h], b_ref[h], state_ref[h], fine, gap, reverse, fast)
            state_ref[h] = st
            o_ref[h] = o.astype(o_ref.dtype)

    blocks = [blk for blk in FAST_BLOCKS if t % blk == 0]
    fits = [spread(blk) <= FAST_RANGE for blk in blocks]
    unresolved = None
    for blk, ok in zip(blocks, fits):
        pl.when(ok if unresolved is None else unresolved & ok)(functools.partial(run, blk))
        unresolved = jnp.logical_not(ok) if unresolved is None else unresolved & jnp.logical_not(ok)
    if unresolved is None:
        run(0)
    else:
        pl.when(unresolved)(functools.partial(run, 0))

    @pl.when(step == n_tiles - 1)
    def _():
        sfin_ref[...] = state_ref[...]


def _scan(q, k, v, bcum, s0, reverse):
    b, h, l, hd = q.shape
    t = _token_tile(l, SCAN_TILE)
    n_tiles = l // t
    hp = SCAN_HEADS
    _, fine, gap = _scan_consts(LANES, reverse)
    tile = (lambda j: n_tiles - 1 - j) if reverse else (lambda j: j)
    tok_spec = pl.BlockSpec((None, hp, t, hd), lambda i, hh, j: (i, hh, tile(j), 0))
    st_spec = pl.BlockSpec((None, hp, hd, hd), lambda i, hh, j: (i, hh, 0, 0))
    return pl.pallas_call(
        functools.partial(_scan_kernel, reverse=reverse, n_tiles=n_tiles),
        grid=(b, h // hp, n_tiles),
        in_specs=[tok_spec, tok_spec, tok_spec, tok_spec, st_spec, _const_spec(fine.shape), _const_spec(gap.shape)],
        out_specs=[tok_spec, st_spec],
        out_shape=[jax.ShapeDtypeStruct((b, h, l, hd), BF16), jax.ShapeDtypeStruct((b, h, hd, hd), F32)],
        scratch_shapes=[pltpu.VMEM((hp, hd, hd), F32)],
        compiler_params=_params("parallel", "parallel", "arbitrary"),
        name="scan_bwd" if reverse else "scan_fwd",
    )(q, k, v, bcum, s0, fine, gap)


def _hgrn_out_ffn_kernel(of_ref, ob_ref, sg_ref, x_ref, mod_ref, gpost_ref, gn_ref, w_ref,
                         fmod_ref, fgpre_ref, fgpost_ref, win_ref, wout_ref, o_ref):
    heads = []
    for h in range(HGRN_HEADS):
        heads.append(_rms(of_ref[h].astype(F32) + ob_ref[h].astype(F32), gn_ref[...]))
    o = jnp.concatenate(heads, axis=1)
    y = _dot((o * sg_ref[...].astype(F32)).astype(BF16), w_ref[...])
    x = _post(x_ref[...], y, mod_ref, gpost_ref, 1.0)
    o_ref[...] = _ffn_step(x, fmod_ref, fgpre_ref, fgpost_ref, win_ref, wout_ref)


def _hgrn_readout_ffn(o_f, o_b, sg, x, mod, gpost, g_norm, w_out, lead, fmod, fgpre, fgpost, w_in, w_ffn, flead):
    b, l, d = x.shape
    hd = d // HGRN_HEADS
    assert w_ffn.shape[-2] % MXU_COLS == 0
    tm = _token_tile(l, 512)
    x_spec = pl.BlockSpec((None, tm, d), lambda i, j: (i, j, 0))
    head_spec = pl.BlockSpec((None, HGRN_HEADS, tm, hd), lambda i, j: (i, 0, j, 0))
    mod_spec = pl.BlockSpec((None, 3, d), lambda i, j: (i, 0, 0))
    return pl.pallas_call(
        _hgrn_out_ffn_kernel,
        grid=(b, l // tm),
        in_specs=[
            head_spec, head_spec, x_spec, x_spec,
            mod_spec,
            _const_spec((1, d)),
            _const_spec((1, hd)),
            _pick_spec(w_out, lead),
            mod_spec,
            _const_spec((1, d)),
            _const_spec((1, d)),
            _pick_spec(w_in, flead),
            _pick_spec(w_ffn, flead),
        ],
        out_specs=x_spec,
        out_shape=jax.ShapeDtypeStruct(x.shape, F32),
        compiler_params=_params("parallel", "parallel"),
        name="hgrn_out_ffn",
    )(o_f, o_b, sg, x, mod, gpost, g_norm, w_out, fmod, fgpre, fgpost, w_in, w_ffn)


def kernel(x, c, ctx, c_ctx, ada_w, ada_b, norm_pre, norm_post, ffn_w_in, ffn_w_out, fourier_w_out,
           hgrn_w_in, hgrn_lb_fwd, hgrn_lb_bwd, hgrn_norm, hgrn_w_out):
    bsz, _, d = x.shape
    depth = ada_w.shape[0]
    assert bsz + 1 <= MOD_ROWS

    c_rows = jnp.concatenate([c, c_ctx[None, :], jnp.zeros((MOD_ROWS - bsz - 1, d), F32)], axis=0)
    mod = _adaln(c_rows, ada_w, ada_b).reshape(depth, MOD_ROWS, N_MOD, d)

    ffn_w_in = ffn_w_in.astype(BF16)
    ffn_w_out = ffn_w_out.astype(BF16)
    fourier_w_out = fourier_w_out.astype(BF16)
    hgrn_w_in = hgrn_w_in.astype(BF16)
    hgrn_w_out = hgrn_w_out.astype(BF16)

    for i in range(depth):
        last = i == depth - 1
        is_hgrn = i % 2 == 1
        jm = i // 2
        mx = lambda j: mod[i, :bsz, 3 * j:3 * j + 3]
        mc = lambda j: mod[i, bsz:bsz + 1, 3 * j:3 * j + 3]
        gpre = lambda j: norm_pre[i, j][None, :]
        gpost = lambda j: norm_post[i, j][None, :]
        ctx_mod = lambda j: jnp.broadcast_to(mc(j), (bsz, 3, d))
        ffn = lambda s, m, j, f: _ffn(s, m(j), gpre(j), gpost(j), ffn_w_in, ffn_w_out, (i, f))

        def ctx_ffn(s, j, f):
            n = s.shape[0] * s.shape[1]
            rows = FFN_TOKENS if n % FFN_TOKENS == 0 else s.shape[1]
            m = jnp.broadcast_to(mc(j), (n // rows, 3, d))
            return _ffn(s.reshape(n // rows, rows, d), m, gpre(j), gpost(j), ffn_w_in, ffn_w_out, (i, f)).reshape(s.shape)
        ctx_first = is_hgrn or not last

        if not is_hgrn:
            x = _ffn_fourier_latent(x, (mx(0), mx(2)), (gpre(0), gpre(2)), (gpost(0), gpost(2)),
                                    ffn_w_in, ffn_w_out, (i, 0), (i, 1),
                                    mx(1), gpre(1), gpost(1), fourier_w_out, (jm,))
            if ctx_first:
                ctx = ctx_ffn(ctx, 0, 0)
            if not last:
                ctx = _fourier_ctx(ctx, mc(1), gpre(1), gpost(1), fourier_w_out, (jm,))
                ctx = ctx_ffn(ctx, 2, 1)
        else:
            x = ffn(x, mx, 0, 0)
            ctx = ctx_ffn(ctx, 0, 0)
            hd = d // HGRN_HEADS
            zero = jnp.zeros((bsz, HGRN_HEADS, hd, hd), F32)
            qc, vc, kfc, bfc, kbc, bbc, sgc = _hgrn_inputs(
                ctx, ctx_mod(1), gpre(1), hgrn_w_in, (jm,), hgrn_lb_fwd, hgrn_lb_bwd, jm)
            ocf, s_f = _scan(qc, kfc, vc, bfc, zero, False)
            ocb, s_b = _scan(qc, kbc, vc, bbc, zero, True)
            qx, vx, kfx, bfx, kbx, bbx, sgx = _hgrn_inputs(
                x, mx(1), gpre(1), hgrn_w_in, (jm,), hgrn_lb_fwd, hgrn_lb_bwd, jm)
            oxf, _ = _scan(qx, kfx, vx, bfx, s_f, False)
            oxb, _ = _scan(qx, kbx, vx, bbx, s_b, True)
            gn = hgrn_norm[jm][None, :]
            x = _hgrn_readout_ffn(oxf, oxb, sgx, x, mx(1), gpost(1), gn, hgrn_w_out, (jm,),
                                  mx(2), gpre(2), gpost(2), ffn_w_in, ffn_w_out, (i, 1))
            if not last:
                ctx = _hgrn_readout_ffn(ocf, ocb, sgc, ctx, ctx_mod(1), gpost(1), gn, hgrn_w_out, (jm,),
                                        ctx_mod(2), gpre(2), gpost(2), ffn_w_in, ffn_w_out, (i, 1))
    return x
```

```python
import functools

import jax
import jax.numpy as jnp
import numpy as np
from jax import lax
from jax.experimental import pallas as pl
from jax.experimental.pallas import tpu as pltpu

F32 = jnp.float32
BF16 = jnp.bfloat16

GRID_W = 64
FOURIER_GROUPS = 4
HGRN_HEADS = 8
N_MOD = 9
HALF = 0.5
NORM_EPS = 1e-6
LB_FLOOR = 1e-30

VMEM_LIMIT_V7X = 56 * 1024 * 1024
SUBLANES = 8
LANES = 128
MXU_COLS = 256
FFN_TOKENS = 1024
MIXER_TOKENS = 512
ADALN_COLS = 1024
SCAN_TILE = 512
FAST_BLOCKS = (128, 64, 32, 16)
FAST_RANGE = 200.0
SCAN_HEADS = 8
MOD_ROWS = 16


def _params(*sem):
    return pltpu.CompilerParams(dimension_semantics=sem, vmem_limit_bytes=VMEM_LIMIT_V7X)


def _const_spec(shape):
    n = len(shape)
    return pl.BlockSpec(shape, lambda *_: (0,) * n, pipeline_mode=pl.Buffered(1))


def _pick_spec(stacked, lead):
    tail = stacked.shape[len(lead):]
    return pl.BlockSpec((None,) * len(lead) + tail, lambda *_: tuple(lead) + (0,) * len(tail),
                        pipeline_mode=pl.Buffered(1))


def _rms(x, g):
    ms = jnp.mean(x * x, axis=-1, keepdims=True)
    return x * lax.rsqrt(ms + NORM_EPS) * g


def _pre(x, mod_ref, g_ref):
    return _rms(x, g_ref[...]) * (1.0 + mod_ref[1:2, :]) + mod_ref[0:1, :]


def _post(x, y, mod_ref, g_ref, w):
    return x + w * mod_ref[2:3, :] * _rms(y, g_ref[...])


def _silu(x):
    return x * jax.nn.sigmoid(x)


def _dot(a, b):
    return jnp.dot(a, b, preferred_element_type=F32)


def _dot_nt(a, b):
    return lax.dot_general(a, b, (((1,), (1,)), ((), ())), preferred_element_type=F32)


def _dot_tn(a, b):
    return lax.dot_general(a, b, (((0,), (0,)), ((), ())), preferred_element_type=F32)


def _adaln_kernel(c_ref, w_ref, b_ref, o_ref):
    sc = _silu(c_ref[...]).astype(BF16)
    o_ref[...] = _dot(sc, w_ref[...].astype(BF16)) + b_ref[...]


def _adaln(c_rows, ada_w, ada_b):
    depth, d, n = ada_w.shape
    tn = ADALN_COLS
    return pl.pallas_call(
        _adaln_kernel,
        grid=(depth, n // tn),
        in_specs=[
            pl.BlockSpec((MOD_ROWS, d), lambda i, j: (0, 0)),
            pl.BlockSpec((None, d, tn), lambda i, j: (i, 0, j)),
            pl.BlockSpec((None, 1, tn), lambda i, j: (i, 0, j)),
        ],
        out_specs=pl.BlockSpec((None, MOD_ROWS, tn), lambda i, j: (i, 0, j)),
        out_shape=jax.ShapeDtypeStruct((depth, MOD_ROWS, n), F32),
        compiler_params=_params("parallel", "parallel"),
        name="adaln",
    )(c_rows, ada_w, ada_b.reshape(depth, 1, n))


def _ffn_step(x, mod_ref, gpre_ref, gpost_ref, win_ref, wout_ref):
    d_ff = wout_ref.shape[0]
    hb = _pre(x, mod_ref, gpre_ref).astype(BF16)
    fc = MXU_COLS
    acc = None
    for ci in range(d_ff // fc):
        gate = _dot(hb, win_ref[:, ci * fc:(ci + 1) * fc])
        up = _dot(hb, win_ref[:, d_ff + ci * fc:d_ff + (ci + 1) * fc])
        a = (_silu(gate) * up).astype(BF16)
        y = _dot(a, wout_ref[ci * fc:(ci + 1) * fc, :])
        acc = y if acc is None else acc + y
    return _post(x, acc, mod_ref, gpost_ref, HALF)


def _ffn_kernel(x_ref, mod_ref, gpre_ref, gpost_ref, win_ref, wout_ref, o_ref):
    o_ref[...] = _ffn_step(x_ref[...], mod_ref, gpre_ref, gpost_ref, win_ref, wout_ref)


def _token_tile(n_tokens, want):
    return want if n_tokens % want == 0 else n_tokens


def _ffn(s, mod, gpre, gpost, w_in, w_out, lead):
    b, l, d = s.shape
    d_ff = w_out.shape[-2]
    assert d_ff % MXU_COLS == 0
    tm = _token_tile(l, FFN_TOKENS)
    return pl.pallas_call(
        _ffn_kernel,
        grid=(b, l // tm),
        in_specs=[
            pl.BlockSpec((None, tm, d), lambda i, j: (i, j, 0)),
            pl.BlockSpec((None, 3, d), lambda i, j: (i, 0, 0)),
            _const_spec((1, d)),
            _const_spec((1, d)),
            _pick_spec(w_in, lead),
            _pick_spec(w_out, lead),
        ],
        out_specs=pl.BlockSpec((None, tm, d), lambda i, j: (i, j, 0)),
        out_shape=jax.ShapeDtypeStruct(s.shape, F32),
        compiler_params=_params("parallel", "parallel"),
        name="ffn",
    )(s, mod, gpre, gpost, w_in, w_out)


def _dft_cos_sin(n):
    k = np.arange(n)
    ang = 2.0 * np.pi * ((k[:, None] * k[None, :]) % n) / n
    s = 1.0 / np.sqrt(n)
    return np.cos(ang) * s, np.sin(ang) * s


def _fourier_consts(gd):
    c, s = _dft_cos_sin(gd)
    chan = np.concatenate([c, s], axis=1)
    c, s = _dft_cos_sin(GRID_W)
    col = np.block([[c, -s], [s, c]])
    return jnp.asarray(chan, BF16), jnp.asarray(col, BF16)


def _channel_dft(hb, chan_ref, gd):
    us, vs = [], []
    for g in range(FOURIER_GROUPS):
        uv = _dot(hb[:, g * gd:(g + 1) * gd], chan_ref[...])
        us.append(uv[:, :gd])
        vs.append(uv[:, gd:])
    return (jnp.concatenate(us, axis=1).astype(BF16), jnp.concatenate(vs, axis=1).astype(BF16))


def _pack_pair(a, b):
    hi = lax.bitcast_convert_type(a.astype(BF16).astype(F32), jnp.uint32)
    lo = lax.bitcast_convert_type(b.astype(BF16).astype(F32), jnp.uint32)
    return hi | (lo >> 16)


def _unpack_pair(p):
    a = lax.bitcast_convert_type(p & jnp.uint32(0xFFFF0000), F32)
    b = lax.bitcast_convert_type(p << 16, F32)
    return a.astype(BF16), b.astype(BF16)


def _ffn_fourier_cols_kernel(x_ref, fmod_ref, fgpre_ref, fgpost_ref, win_ref, wout_ref,
                             mod_ref, gpre_ref, chan_ref, col_ref, o_ref, ab_ref, *, gd):
    x = _ffn_step(x_ref[...], fmod_ref, fgpre_ref, fgpost_ref, win_ref, wout_ref)
    o_ref[...] = x
    hb = _pre(x, mod_ref, gpre_ref).astype(BF16)
    u, v = _channel_dft(hb, chan_ref, gd)
    for r in range(x.shape[0] // GRID_W):
        sl = slice(r * GRID_W, (r + 1) * GRID_W)
        ab = _dot(col_ref[...], jnp.concatenate([u[sl], v[sl]], axis=0))
        ab_ref[sl, :] = _pack_pair(ab[:GRID_W], ab[GRID_W:])


def _fourier_rows_ffn_kernel(ab_ref, x_ref, mod_ref, gpost_ref, rowk_ref, w_ref,
                             fmod_ref, fgpre_ref, fgpost_ref, win_ref, wout_ref, o_ref):
    rows, cps, d = x_ref.shape
    flat = lambda v: v.reshape(rows * cps, d)
    a, b = _unpack_pair(flat(ab_ref[...]))
    y = _dot(rowk_ref[...], jnp.concatenate([a, b], axis=0)).astype(BF16)
    x = _post(flat(x_ref[...]), _dot(y, w_ref[...]), mod_ref, gpost_ref, 1.0)
    o_ref[...] = _ffn_step(x, fmod_ref, fgpre_ref, fgpost_ref, win_ref, wout_ref).reshape(rows, cps, d)


def _ffn_fourier_latent(x, fmod, fgpre, fgpost, w_in, w_ffn, lead_a, lead_b, mod, gpre, gpost, w_out, lead):
    b, l, d = x.shape
    rows = l // GRID_W
    gd = d // FOURIER_GROUPS
    assert w_ffn.shape[-2] % MXU_COLS == 0
    chan, col = _fourier_consts(gd)
    tm = _token_tile(l, MIXER_TOKENS)
    assert tm % GRID_W == 0
    tok = pl.BlockSpec((None, tm, d), lambda i, j: (i, j, 0))
    mod_spec = pl.BlockSpec((None, 3, d), lambda i, j: (i, 0, 0))
    x, ab = pl.pallas_call(
        functools.partial(_ffn_fourier_cols_kernel, gd=gd),
        grid=(b, l // tm),
        in_specs=[
            tok, mod_spec, _const_spec((1, d)), _const_spec((1, d)),
            _pick_spec(w_in, lead_a), _pick_spec(w_ffn, lead_a),
            mod_spec, _const_spec((1, d)), _const_spec(chan.shape), _const_spec(col.shape),
        ],
        out_specs=[tok, tok],
        out_shape=[jax.ShapeDtypeStruct((b, l, d), F32), jax.ShapeDtypeStruct((b, l, d), jnp.uint32)],
        compiler_params=_params("parallel", "parallel"),
        name="ffn_fourier_cols",
    )(x, fmod[0], fgpre[0], fgpost[0], w_in, w_ffn, mod, gpre, chan, col)

    c, s = _dft_cos_sin(rows)
    cps = SUBLANES
    eye = np.eye(cps)
    rowk = jnp.asarray(np.concatenate([np.kron(c, eye), -np.kron(s, eye)], axis=1), BF16)
    grid_view = lambda t: t.reshape(b, rows, GRID_W, d)
    blk = pl.BlockSpec((None, rows, cps, d), lambda i, j: (i, 0, j, 0))
    out = pl.pallas_call(
        _fourier_rows_ffn_kernel,
        grid=(b, GRID_W // cps),
        in_specs=[
            blk, blk, mod_spec, _const_spec((1, d)), _const_spec(rowk.shape), _pick_spec(w_out, lead),
            mod_spec, _const_spec((1, d)), _const_spec((1, d)),
            _pick_spec(w_in, lead_b), _pick_spec(w_ffn, lead_b),
        ],
        out_specs=blk,
        out_shape=jax.ShapeDtypeStruct((b, rows, GRID_W, d), F32),
        compiler_params=_params("parallel", "parallel"),
        name="fourier_rows_ffn",
    )(grid_view(ab), grid_view(x), mod, gpost, rowk, w_out, fmod[1], fgpre[1], fgpost[1], w_in, w_ffn)
    return out.reshape(b, l, d)


def _fourier_ctx_kernel(x_ref, mod_ref, gpre_ref, gpost_ref, chan_ref, seq_ref, w_ref, o_ref, *, gd):
    x = x_ref[...]
    hb = _pre(x, mod_ref, gpre_ref).astype(BF16)
    u, v = _channel_dft(hb, chan_ref, gd)
    y = _dot(seq_ref[...], jnp.concatenate([u, v], axis=0)).astype(BF16)
    o_ref[...] = _post(x, _dot(y, w_ref[...]), mod_ref, gpost_ref, 1.0)


def _fourier_ctx(x, mod, gpre, gpost, w_out, lead):
    b, l, d = x.shape
    gd = d // FOURIER_GROUPS
    chan, _ = _fourier_consts(gd)
    c, s = _dft_cos_sin(l)
    seqm = jnp.asarray(np.concatenate([c, -s], axis=1), BF16)
    return pl.pallas_call(
        functools.partial(_fourier_ctx_kernel, gd=gd),
        grid=(b,),
        in_specs=[
            pl.BlockSpec((None, l, d), lambda i: (i, 0, 0)),
            pl.BlockSpec((None, 3, d), lambda i: (0, 0, 0)),
            _const_spec((1, d)),
            _const_spec((1, d)),
            _const_spec(chan.shape),
            _const_spec(seqm.shape),
            _pick_spec(w_out, lead),
        ],
        out_specs=pl.BlockSpec((None, l, d), lambda i: (i, 0, 0)),
        out_shape=jax.ShapeDtypeStruct(x.shape, F32),
        compiler_params=_params("parallel"),
        name="fourier_ctx",
    )(x, mod, gpre, gpost, chan, seqm, w_out)


def _lower_bound(logits_ref, j):
    lg = logits_ref[...]
    e = jnp.exp(lg - jnp.max(lg, axis=0, keepdims=True))
    p = e / jnp.sum(e, axis=0, keepdims=True)
    lb = jnp.zeros_like(p[0:1])
    for i in range(1, j + 1):
        lb = lb + p[i:i + 1]
    return lb


def _forget_gate(z, lb):
    e = jnp.exp(-jnp.abs(z))
    inv = 1.0 / (1.0 + e)
    pos = z >= 0.0
    sig = jnp.where(pos, inv, e * inv)
    nsig = jnp.where(pos, e * inv, inv)
    f = jnp.maximum(lb, LB_FLOOR) + (1.0 - lb) * sig
    return (1.0 - lb) * nsig, jnp.log2(f)


def _split2(x):
    hi = x.astype(BF16)
    lo = (x - hi.astype(F32)).astype(BF16)
    return hi, lo


def _tile_cumsum(g, tri_ref, tile, reverse):
    c = tri_ref.shape[0]
    parts = _split2(g)
    tri2 = jnp.concatenate([tri_ref[...]] * len(parts), axis=1)
    chunks = [_dot(tri2, jnp.concatenate([p[i:i + c] for p in parts], axis=0)) for i in range(0, g.shape[0], c)]
    per_tile = tile // c
    order = range(per_tile - 1, -1, -1) if reverse else range(per_tile)
    last = 0 if reverse else c - 1
    for t0 in range(0, len(chunks), per_tile):
        carry = None
        for i in order:
            if carry is not None:
                chunks[t0 + i] = chunks[t0 + i] + carry
            carry = chunks[t0 + i][last:last + 1, :]
    return jnp.concatenate(chunks, axis=0)


def _hgrn_in_kernel(x_ref, mod_ref, gpre_ref, w_ref, lbf_ref, lbb_ref, trif_ref, trib_ref,
                    q_ref, v_ref, kf_ref, bf_ref, kb_ref, bb_ref, sg_ref, *, layer_j, kd, hd, tile):
    hb = _pre(x_ref[...], mod_ref, gpre_ref).astype(BF16)
    col = lambda n: _dot(hb, w_ref[:, n * kd:(n + 1) * kd])
    q = _silu(col(0))
    v = col(1)
    kf, lf = _forget_gate(col(2), _lower_bound(lbf_ref, layer_j))
    kb, lbw = _forget_gate(col(3), _lower_bound(lbb_ref, layer_j))
    sg_ref[...] = _silu(col(4)).astype(BF16)
    bf = _tile_cumsum(lf, trif_ref, tile, False)
    bb = _tile_cumsum(lbw, trib_ref, tile, True)
    for h in range(HGRN_HEADS):
        sl = slice(h * hd, (h + 1) * hd)
        q_ref[h] = q[:, sl].astype(BF16)
        v_ref[h] = v[:, sl].astype(BF16)
        kf_ref[h] = kf[:, sl].astype(BF16)
        kb_ref[h] = kb[:, sl].astype(BF16)
        bf_ref[h] = bf[:, sl]
        bb_ref[h] = bb[:, sl]


def _hgrn_inputs(x, mod, gpre, w_in, lead, lb_fwd, lb_bwd, layer_j):
    b, l, d = x.shape
    kd = d
    hd = kd // HGRN_HEADS
    tm = _token_tile(l, MIXER_TOKENS)
    tile = _token_tile(tm, SCAN_TILE)
    t = _token_tile(tile, MXU_COLS)
    trif = _scan_consts(t, False)[0]
    trib = _scan_consts(t, True)[0]
    x_spec = pl.BlockSpec((None, tm, d), lambda i, j: (i, j, 0))
    head_spec = pl.BlockSpec((None, HGRN_HEADS, tm, hd), lambda i, j: (i, 0, j, 0))
    heads = lambda dt: jax.ShapeDtypeStruct((b, HGRN_HEADS, l, hd), dt)
    return pl.pallas_call(
        functools.partial(_hgrn_in_kernel, layer_j=layer_j, kd=kd, hd=hd, tile=tile),
        grid=(b, l // tm),
        in_specs=[
            x_spec,
            pl.BlockSpec((None, 3, d), lambda i, j: (i, 0, 0)),
            _const_spec((1, d)),
            _pick_spec(w_in, lead),
            _const_spec(lb_fwd.shape),
            _const_spec(lb_bwd.shape),
            _const_spec((t, t)),
            _const_spec((t, t)),
        ],
        out_specs=[head_spec] * 6 + [x_spec],
        out_shape=[heads(BF16), heads(BF16), heads(BF16), heads(F32), heads(BF16), heads(F32),
                   jax.ShapeDtypeStruct((b, l, d), BF16)],
        compiler_params=_params("parallel", "parallel"),
        name="hgrn_in",
    )(x, mod, gpre, w_in, lb_fwd, lb_bwd, trif, trib)


def _scan_consts(n, reverse):
    r = np.arange(n)[:, None]
    c = np.arange(n)[None, :]
    valid = (c >= r) if reverse else (c <= r)
    x = r ^ c
    lvl = np.where(x == 0, 0, np.floor(np.log2(np.maximum(x, 1))).astype(np.int64) + 1)
    fine = np.where(valid & (x < SUBLANES), lvl, -1)
    gap = np.where(valid, x, n)
    return jnp.asarray(valid, BF16), jnp.asarray(fine, jnp.int32), jnp.asarray(gap, jnp.int32)


def _level_ref(b, m, reverse):
    t, n = b.shape
    first = m if reverse else m - 1
    b3 = b.reshape(t // SUBLANES, SUBLANES, n)
    rid = lax.broadcasted_iota(jnp.int32, b3.shape, 1)
    ref = None
    for p in range(0, SUBLANES, 2 * m):
        row = jnp.broadcast_to(b3[:, p + first:p + first + 1, :], b3.shape)
        ref = row if ref is None else jnp.where(rid >= p, row, ref)
    return ref.reshape(t, n)


def _neg_abs(d):
    bits = lax.bitcast_convert_type(d, jnp.int32) | jnp.int32(-2 ** 31)
    return lax.bitcast_convert_type(bits, F32)


def _scaled(z, e):
    return (z * e).astype(BF16)


def _halves(lo, m, reverse):
    return (lo, lo + m, lo + m) if reverse else (lo + m, lo, lo + m - 1)


def _strip_scores(q, k, q32, k32, b, fine, gap, lane, reverse, fast):
    n = q.shape[0]
    if fast:
        mid = lambda i: 0.5 * (b[i:i + 1, :] + b[i + fast - 1:i + fast, :])
        ref = jnp.concatenate([jnp.broadcast_to(mid(i), (fast, b.shape[1])) for i in range(0, n, fast)], axis=0)
        d = b - ref
        a = jnp.where(gap < fast, _dot_nt(_scaled(q32, jnp.exp2(d)), _scaled(k32, jnp.exp2(-d))), 0.0)
        m = fast
    else:
        a = jnp.where(fine == 0, _dot_nt(q, k), 0.0)
        before = pltpu.roll(b, (n - 1) if reverse else 1, 0)
        a = jnp.where(fine == 1, _dot_nt(_scaled(q32, jnp.exp2(_neg_abs(b - before))), k), a)
        m, level = 2, 2
        while m < SUBLANES:
            e = jnp.exp2(_neg_abs(b - _level_ref(b, m, reverse)))
            a = jnp.where(fine == level, _dot_nt(_scaled(q32, e), _scaled(k32, e)), a)
            m, level = 2 * m, level + 1
    blocks = [a[i:i + SUBLANES] for i in range(0, n, SUBLANES)]
    while m < n:
        ql, kf = [], []
        for lo in range(0, n, 2 * m):
            late, early, first = _halves(lo, m, reverse)
            ref = b[first:first + 1, :]
            ql.append(_scaled(q32[late:late + m], jnp.exp2(b[late:late + m] - ref)))
            ke = _scaled(k32[early:early + m], jnp.exp2(ref - b[early:early + m]))
            kf += [k[late:late + m], ke] if reverse else [ke, k[late:late + m]]
        s_m = _dot_nt(jnp.concatenate(ql, axis=0), jnp.concatenate(kf, axis=0))
        for pi, lo in enumerate(range(0, n, 2 * m)):
            late, early, _ = _halves(lo, m, reverse)
            inside = (lane >= early) & (lane < early + m)
            for i in range(0, m, SUBLANES):
                rb = (late + i) // SUBLANES
                blocks[rb] = jnp.where(inside, s_m[pi * m + i:pi * m + i + SUBLANES], blocks[rb])
        m *= 2
    return jnp.concatenate(blocks, axis=0)


def _scan_tile(q, k, v, b, st, fine, gap, reverse, fast):
    t = q.shape[0]
    n = fine.shape[0]
    q32 = q.astype(F32)
    k32 = k.astype(F32)
    end = 0 if reverse else t - 1
    b_end = b[end:end + 1, :]

    o = _dot_nt(_scaled(q32, jnp.exp2(b)), st.astype(BF16))
    new_st = st * jnp.exp2(b_end) + _dot_tn(v, _scaled(k32, jnp.exp2(b_end - b)))

    lane = lax.broadcasted_iota(jnp.int32, (SUBLANES, n), 1)
    strips = range(0, t, n)
    score = {(i, i): _strip_scores(q[i:i + n], k[i:i + n], q32[i:i + n], k32[i:i + n], b[i:i + n],
                                   fine, gap, lane, reverse, fast) for i in strips}
    m = n
    while m < t:
        for lo in range(0, t, 2 * m):
            late, early, first = _halves(lo, m, reverse)
            ref = b[first:first + 1, :]
            ql = _scaled(q32[late:late + m], jnp.exp2(b[late:late + m] - ref))
            ke = _scaled(k32[early:early + m], jnp.exp2(ref - b[early:early + m]))
            s_m = _dot_nt(ql, ke)
            for i in range(0, m, n):
                for j in range(0, m, n):
                    score[(late + i, early + j)] = s_m[i:i + n, j:j + n]
        m *= 2
    outs = []
    for i in strips:
        keys = [j for j in strips if (i, j) in score]
        a = jnp.concatenate([score[(i, j)] for j in keys], axis=1).astype(BF16)
        outs.append(_dot(a, jnp.concatenate([v[j:j + n] for j in keys], axis=0)))
    return o + jnp.concatenate(outs, axis=0), new_st


def _scan_kernel(q_ref, k_ref, v_ref, b_ref, s0_ref, fine_ref, gap_ref, o_ref, sfin_ref, state_ref,
                 *, reverse, n_tiles):
    step = pl.program_id(2)

    @pl.when(step == 0)
    def _():
        state_ref[...] = s0_ref[...]

    hp, t, _ = q_ref.shape

    def spread(blk):
        worst = None
        for h in range(hp):
            top = b_ref[h, pl.ds(0, t // blk, stride=blk), :]
            bottom = b_ref[h, pl.ds(blk - 1, t // blk, stride=blk), :]
            s = jnp.abs(top - bottom)
            worst = s if worst is None else jnp.maximum(worst, s)
        return jnp.max(worst)

    def run(fast):
        fine = fine_ref[...]
        gap = gap_ref[...]
        for h in range(hp):
            o, st = _scan_tile(q_ref[h], k_ref[h], v_ref[h], b_ref[h], state_ref[h], fine, gap, reverse, fast)
            state_ref[h] = st
            o_ref[h] = o.astype(o_ref.dtype)

    blocks = [blk for blk in FAST_BLOCKS if t % blk == 0]
    fits = [spread(blk) <= FAST_RANGE for blk in blocks]
    unresolved = None
    for blk, ok in zip(blocks, fits):
        pl.when(ok if unresolved is None else unresolved & ok)(functools.partial(run, blk))
        unresolved = jnp.logical_not(ok) if unresolved is None else unresolved & jnp.logical_not(ok)
    if unresolved is None:
        run(0)
    else:
        pl.when(unresolved)(functools.partial(run, 0))

    @pl.when(step == n_tiles - 1)
    def _():
        sfin_ref[...] = state_ref[...]


def _scan(q, k, v, bcum, s0, reverse):
    b, h, l, hd = q.shape
    t = _token_tile(l, SCAN_TILE)
    n_tiles = l // t
    hp = SCAN_HEADS
    _, fine, gap = _scan_consts(LANES, reverse)
    tile = (lambda j: n_tiles - 1 - j) if reverse else (lambda j: j)
    tok_spec = pl.BlockSpec((None, hp, t, hd), lambda i, hh, j: (i, hh, tile(j), 0))
    st_spec = pl.BlockSpec((None, hp, hd, hd), lambda i, hh, j: (i, hh, 0, 0))
    return pl.pallas_call(
        functools.partial(_scan_kernel, reverse=reverse, n_tiles=n_tiles),
        grid=(b, h // hp, n_tiles),
        in_specs=[tok_spec, tok_spec, tok_spec, tok_spec, st_spec, _const_spec(fine.shape), _const_spec(gap.shape)],
        out_specs=[tok_spec, st_spec],
        out_shape=[jax.ShapeDtypeStruct((b, h, l, hd), BF16), jax.ShapeDtypeStruct((b, h, hd, hd), F32)],
        scratch_shapes=[pltpu.VMEM((hp, hd, hd), F32)],
        compiler_params=_params("parallel", "parallel", "arbitrary"),
        name="scan_bwd" if reverse else "scan_fwd",
    )(q, k, v, bcum, s0, fine, gap)


def _hgrn_out_ffn_kernel(of_ref, ob_ref, sg_ref, x_ref, mod_ref, gpost_ref, gn_ref, w_ref,
                         fmod_ref, fgpre_ref, fgpost_ref, win_ref, wout_ref, o_ref):
    heads = []
    for h in range(HGRN_HEADS):
        heads.append(_rms(of_ref[h].astype(F32) + ob_ref[h].astype(F32), gn_ref[...]))
    o = jnp.concatenate(heads, axis=1)
    y = _dot((o * sg_ref[...].astype(F32)).astype(BF16), w_ref[...])
    x = _post(x_ref[...], y, mod_ref, gpost_ref, 1.0)
    o_ref[...] = _ffn_step(x, fmod_ref, fgpre_ref, fgpost_ref, win_ref, wout_ref)


def _hgrn_readout_ffn(o_f, o_b, sg, x, mod, gpost, g_norm, w_out, lead, fmod, fgpre, fgpost, w_in, w_ffn, flead):
    b, l, d = x.shape
    hd = d // HGRN_HEADS
    assert w_ffn.shape[-2] % MXU_COLS == 0
    tm = _token_tile(l, MIXER_TOKENS)
    x_spec = pl.BlockSpec((None, tm, d), lambda i, j: (i, j, 0))
    head_spec = pl.BlockSpec((None, HGRN_HEADS, tm, hd), lambda i, j: (i, 0, j, 0))
    mod_spec = pl.BlockSpec((None, 3, d), lambda i, j: (i, 0, 0))
    return pl.pallas_call(
        _hgrn_out_ffn_kernel,
        grid=(b, l // tm),
        in_specs=[
            head_spec, head_spec, x_spec, x_spec,
            mod_spec,
            _const_spec((1, d)),
            _const_spec((1, hd)),
            _pick_spec(w_out, lead),
            mod_spec,
            _const_spec((1, d)),
            _const_spec((1, d)),
            _pick_spec(w_in, flead),
            _pick_spec(w_ffn, flead),
        ],
        out_specs=x_spec,
        out_shape=jax.ShapeDtypeStruct(x.shape, F32),
        compiler_params=_params("parallel", "parallel"),
        name="hgrn_out_ffn",
    )(o_f, o_b, sg, x, mod, gpost, g_norm, w_out, fmod, fgpre, fgpost, w_in, w_ffn)


def kernel(x, c, ctx, c_ctx, ada_w, ada_b, norm_pre, norm_post, ffn_w_in, ffn_w_out, fourier_w_out,
           hgrn_w_in, hgrn_lb_fwd, hgrn_lb_bwd, hgrn_norm, hgrn_w_out):
    bsz, _, d = x.shape
    depth = ada_w.shape[0]
    assert bsz + 1 <= MOD_ROWS

    c_rows = jnp.concatenate([c, c_ctx[None, :], jnp.zeros((MOD_ROWS - bsz - 1, d), F32)], axis=0)
    mod = _adaln(c_rows, ada_w, ada_b).reshape(depth, MOD_ROWS, N_MOD, d)

    ffn_w_in = ffn_w_in.astype(BF16)
    ffn_w_out = ffn_w_out.astype(BF16)
    fourier_w_out = fourier_w_out.astype(BF16)
    hgrn_w_in = hgrn_w_in.astype(BF16)
    hgrn_w_out = hgrn_w_out.astype(BF16)

    for i in range(depth):
        last = i == depth - 1
        is_hgrn = i % 2 == 1
        jm = i // 2
        mx = lambda j: mod[i, :bsz, 3 * j:3 * j + 3]
        mc = lambda j: mod[i, bsz:bsz + 1, 3 * j:3 * j + 3]
        gpre = lambda j: norm_pre[i, j][None, :]
        gpost = lambda j: norm_post[i, j][None, :]
        ctx_mod = lambda j: jnp.broadcast_to(mc(j), (bsz, 3, d))
        ffn = lambda s, m, j, f: _ffn(s, m(j), gpre(j), gpost(j), ffn_w_in, ffn_w_out, (i, f))

        def ctx_ffn(s, j, f):
            n = s.shape[0] * s.shape[1]
            rows = FFN_TOKENS if n % FFN_TOKENS == 0 else s.shape[1]
            m = jnp.broadcast_to(mc(j), (n // rows, 3, d))
            return _ffn(s.reshape(n // rows, rows, d), m, gpre(j), gpost(j), ffn_w_in, ffn_w_out, (i, f)).reshape(s.shape)

        if not is_hgrn:
            x = _ffn_fourier_latent(x, (mx(0), mx(2)), (gpre(0), gpre(2)), (gpost(0), gpost(2)),
                                    ffn_w_in, ffn_w_out, (i, 0), (i, 1),
                                    mx(1), gpre(1), gpost(1), fourier_w_out, (jm,))
            if not last:
                ctx = ctx_ffn(ctx, 0, 0)
                ctx = _fourier_ctx(ctx, mc(1), gpre(1), gpost(1), fourier_w_out, (jm,))
                ctx = ctx_ffn(ctx, 2, 1)
        else:
            x = ffn(x, mx, 0, 0)
            ctx = ctx_ffn(ctx, 0, 0)
            hd = d // HGRN_HEADS
            zero = jnp.zeros((bsz, HGRN_HEADS, hd, hd), F32)
            qc, vc, kfc, bfc, kbc, bbc, sgc = _hgrn_inputs(
                ctx, ctx_mod(1), gpre(1), hgrn_w_in, (jm,), hgrn_lb_fwd, hgrn_lb_bwd, jm)
            ocf, s_f = _scan(qc, kfc, vc, bfc, zero, False)
            ocb, s_b = _scan(qc, kbc, vc, bbc, zero, True)
            qx, vx, kfx, bfx, kbx, bbx, sgx = _hgrn_inputs(
                x, mx(1), gpre(1), hgrn_w_in, (jm,), hgrn_lb_fwd, hgrn_lb_bwd, jm)
            oxf, _ = _scan(qx, kfx, vx, bfx, s_f, False)
            oxb, _ = _scan(qx, kbx, vx, bbx, s_b, True)
            gn = hgrn_norm[jm][None, :]
            x = _hgrn_readout_ffn(oxf, oxb, sgx, x, mx(1), gpost(1), gn, hgrn_w_out, (jm,),
                                  mx(2), gpre(2), gpost(2), ffn_w_in, ffn_w_out, (i, 1))
            if not last:
                ctx = _hgrn_readout_ffn(ocf, ocb, sgc, ctx, ctx_mod(1), gpost(1), gn, hgrn_w_out, (jm,),
                                        ctx_mod(2), gpre(2), gpost(2), ffn_w_in, ffn_w_out, (i, 1))
    return x
```

```python
import functools

import jax
import jax.numpy as jnp
import numpy as np
from jax import lax
from jax.experimental import pallas as pl
from jax.experimental.pallas import tpu as pltpu

F32 = jnp.float32
BF16 = jnp.bfloat16

GRID_W = 64
FOURIER_GROUPS = 4
HGRN_HEADS = 8
N_MOD = 9
HALF = 0.5
NORM_EPS = 1e-6
LB_FLOOR = 1e-30

VMEM_LIMIT_V7X = 56 * 1024 * 1024
SUBLANES = 8
LANES = 128
MXU_COLS = 256
FFN_TOKENS = 1024
MIXER_TOKENS = 512
ADALN_COLS = 1024
SCAN_TILE = 512
FAST_BLOCKS = (128, 64, 32, 16)
FAST_RANGE = 200.0
SCAN_HEADS = 8
SCAN_BATCH = 2
MOD_ROWS = 16


def _params(*sem):
    return pltpu.CompilerParams(dimension_semantics=sem, vmem_limit_bytes=VMEM_LIMIT_V7X)


def _const_spec(shape):
    n = len(shape)
    return pl.BlockSpec(shape, lambda *_: (0,) * n, pipeline_mode=pl.Buffered(1))


def _pick_spec(stacked, lead):
    tail = stacked.shape[len(lead):]
    return pl.BlockSpec((None,) * len(lead) + tail, lambda *_: tuple(lead) + (0,) * len(tail),
                        pipeline_mode=pl.Buffered(1))


def _rms(x, g):
    ms = jnp.mean(x * x, axis=-1, keepdims=True)
    return x * lax.rsqrt(ms + NORM_EPS) * g


def _pre(x, mod_ref, g_ref):
    return _rms(x, g_ref[...]) * (1.0 + mod_ref[1:2, :]) + mod_ref[0:1, :]


def _post(x, y, mod_ref, g_ref, w):
    return x + w * mod_ref[2:3, :] * _rms(y, g_ref[...])


def _silu(x):
    return x * jax.nn.sigmoid(x)


def _dot(a, b):
    return jnp.dot(a, b, preferred_element_type=F32)


def _dot_nt(a, b):
    return lax.dot_general(a, b, (((1,), (1,)), ((), ())), preferred_element_type=F32)


def _dot_tn(a, b):
    return lax.dot_general(a, b, (((0,), (0,)), ((), ())), preferred_element_type=F32)


def _adaln_kernel(c_ref, w_ref, b_ref, o_ref):
    sc = _silu(c_ref[...]).astype(BF16)
    o_ref[...] = _dot(sc, w_ref[...].astype(BF16)) + b_ref[...]


def _adaln(c_rows, ada_w, ada_b):
    depth, d, n = ada_w.shape
    tn = ADALN_COLS
    return pl.pallas_call(
        _adaln_kernel,
        grid=(depth, n // tn),
        in_specs=[
            pl.BlockSpec((MOD_ROWS, d), lambda i, j: (0, 0)),
            pl.BlockSpec((None, d, tn), lambda i, j: (i, 0, j)),
            pl.BlockSpec((None, 1, tn), lambda i, j: (i, 0, j)),
        ],
        out_specs=pl.BlockSpec((None, MOD_ROWS, tn), lambda i, j: (i, 0, j)),
        out_shape=jax.ShapeDtypeStruct((depth, MOD_ROWS, n), F32),
        compiler_params=_params("parallel", "parallel"),
        name="adaln",
    )(c_rows, ada_w, ada_b.reshape(depth, 1, n))


def _ffn_step(x, mod_ref, gpre_ref, gpost_ref, win_ref, wout_ref):
    d_ff = wout_ref.shape[0]
    hb = _pre(x, mod_ref, gpre_ref).astype(BF16)
    fc = MXU_COLS
    acc = None
    for ci in range(d_ff // fc):
        gate = _dot(hb, win_ref[:, ci * fc:(ci + 1) * fc])
        up = _dot(hb, win_ref[:, d_ff + ci * fc:d_ff + (ci + 1) * fc])
        a = (_silu(gate) * up).astype(BF16)
        y = _dot(a, wout_ref[ci * fc:(ci + 1) * fc, :])
        acc = y if acc is None else acc + y
    return _post(x, acc, mod_ref, gpost_ref, HALF)


def _ffn_kernel(x_ref, mod_ref, gpre_ref, gpost_ref, win_ref, wout_ref, o_ref):
    o_ref[...] = _ffn_step(x_ref[...], mod_ref, gpre_ref, gpost_ref, win_ref, wout_ref)


def _token_tile(n_tokens, want):
    return want if n_tokens % want == 0 else n_tokens


def _ffn(s, mod, gpre, gpost, w_in, w_out, lead):
    b, l, d = s.shape
    d_ff = w_out.shape[-2]
    assert d_ff % MXU_COLS == 0
    tm = _token_tile(l, FFN_TOKENS)
    return pl.pallas_call(
        _ffn_kernel,
        grid=(b, l // tm),
        in_specs=[
            pl.BlockSpec((None, tm, d), lambda i, j: (i, j, 0)),
            pl.BlockSpec((None, 3, d), lambda i, j: (i, 0, 0)),
            _const_spec((1, d)),
            _const_spec((1, d)),
            _pick_spec(w_in, lead),
            _pick_spec(w_out, lead),
        ],
        out_specs=pl.BlockSpec((None, tm, d), lambda i, j: (i, j, 0)),
        out_shape=jax.ShapeDtypeStruct(s.shape, F32),
        compiler_params=_params("parallel", "parallel"),
        name="ffn",
    )(s, mod, gpre, gpost, w_in, w_out)


def _dft_cos_sin(n):
    k = np.arange(n)
    ang = 2.0 * np.pi * ((k[:, None] * k[None, :]) % n) / n
    s = 1.0 / np.sqrt(n)
    return np.cos(ang) * s, np.sin(ang) * s


def _fourier_consts(gd):
    c, s = _dft_cos_sin(gd)
    chan = np.concatenate([c, s], axis=1)
    c, s = _dft_cos_sin(GRID_W)
    col = np.block([[c, -s], [s, c]])
    return jnp.asarray(chan, BF16), jnp.asarray(col, BF16)


def _channel_dft(hb, chan_ref, gd):
    us, vs = [], []
    for g in range(FOURIER_GROUPS):
        uv = _dot(hb[:, g * gd:(g + 1) * gd], chan_ref[...])
        us.append(uv[:, :gd])
        vs.append(uv[:, gd:])
    return (jnp.concatenate(us, axis=1).astype(BF16), jnp.concatenate(vs, axis=1).astype(BF16))


def _pack_pair(a, b):
    hi = lax.bitcast_convert_type(a.astype(BF16).astype(F32), jnp.uint32)
    lo = lax.bitcast_convert_type(b.astype(BF16).astype(F32), jnp.uint32)
    return hi | (lo >> 16)


def _unpack_pair(p):
    a = lax.bitcast_convert_type(p & jnp.uint32(0xFFFF0000), F32)
    b = lax.bitcast_convert_type(p << 16, F32)
    return a.astype(BF16), b.astype(BF16)


def _ffn_fourier_cols_kernel(x_ref, fmod_ref, fgpre_ref, fgpost_ref, win_ref, wout_ref,
                             mod_ref, gpre_ref, chan_ref, col_ref, o_ref, ab_ref, *, gd):
    x = _ffn_step(x_ref[...], fmod_ref, fgpre_ref, fgpost_ref, win_ref, wout_ref)
    o_ref[...] = x
    hb = _pre(x, mod_ref, gpre_ref).astype(BF16)
    u, v = _channel_dft(hb, chan_ref, gd)
    for r in range(x.shape[0] // GRID_W):
        sl = slice(r * GRID_W, (r + 1) * GRID_W)
        ab = _dot(col_ref[...], jnp.concatenate([u[sl], v[sl]], axis=0))
        ab_ref[sl, :] = _pack_pair(ab[:GRID_W], ab[GRID_W:])


def _fourier_rows_ffn_kernel(ab_ref, x_ref, mod_ref, gpost_ref, rowk_ref, w_ref,
                             fmod_ref, fgpre_ref, fgpost_ref, win_ref, wout_ref, o_ref):
    rows, cps, d = x_ref.shape
    flat = lambda v: v.reshape(rows * cps, d)
    a, b = _unpack_pair(flat(ab_ref[...]))
    y = _dot(rowk_ref[...], jnp.concatenate([a, b], axis=0)).astype(BF16)
    x = _post(flat(x_ref[...]), _dot(y, w_ref[...]), mod_ref, gpost_ref, 1.0)
    o_ref[...] = _ffn_step(x, fmod_ref, fgpre_ref, fgpost_ref, win_ref, wout_ref).reshape(rows, cps, d)


def _ffn_fourier_latent(x, fmod, fgpre, fgpost, w_in, w_ffn, lead_a, lead_b, mod, gpre, gpost, w_out, lead):
    b, l, d = x.shape
    rows = l // GRID_W
    gd = d // FOURIER_GROUPS
    assert w_ffn.shape[-2] % MXU_COLS == 0
    chan, col = _fourier_consts(gd)
    tm = _token_tile(l, MIXER_TOKENS)
    assert tm % GRID_W == 0
    tok = pl.BlockSpec((None, tm, d), lambda i, j: (i, j, 0))
    mod_spec = pl.BlockSpec((None, 3, d), lambda i, j: (i, 0, 0))
    x, ab = pl.pallas_call(
        functools.partial(_ffn_fourier_cols_kernel, gd=gd),
        grid=(b, l // tm),
        in_specs=[
            tok, mod_spec, _const_spec((1, d)), _const_spec((1, d)),
            _pick_spec(w_in, lead_a), _pick_spec(w_ffn, lead_a),
            mod_spec, _const_spec((1, d)), _const_spec(chan.shape), _const_spec(col.shape),
        ],
        out_specs=[tok, tok],
        out_shape=[jax.ShapeDtypeStruct((b, l, d), F32), jax.ShapeDtypeStruct((b, l, d), jnp.uint32)],
        compiler_params=_params("parallel", "parallel"),
        name="ffn_fourier_cols",
    )(x, fmod[0], fgpre[0], fgpost[0], w_in, w_ffn, mod, gpre, chan, col)

    c, s = _dft_cos_sin(rows)
    cps = SUBLANES
    eye = np.eye(cps)
    rowk = jnp.asarray(np.concatenate([np.kron(c, eye), -np.kron(s, eye)], axis=1), BF16)
    grid_view = lambda t: t.reshape(b, rows, GRID_W, d)
    blk = pl.BlockSpec((None, rows, cps, d), lambda i, j: (i, 0, j, 0))
    out = pl.pallas_call(
        _fourier_rows_ffn_kernel,
        grid=(b, GRID_W // cps),
        in_specs=[
            blk, blk, mod_spec, _const_spec((1, d)), _const_spec(rowk.shape), _pick_spec(w_out, lead),
            mod_spec, _const_spec((1, d)), _const_spec((1, d)),
            _pick_spec(w_in, lead_b), _pick_spec(w_ffn, lead_b),
        ],
        out_specs=blk,
        out_shape=jax.ShapeDtypeStruct((b, rows, GRID_W, d), F32),
        compiler_params=_params("parallel", "parallel"),
        name="fourier_rows_ffn",
    )(grid_view(ab), grid_view(x), mod, gpost, rowk, w_out, fmod[1], fgpre[1], fgpost[1], w_in, w_ffn)
    return out.reshape(b, l, d)


def _fourier_ctx_kernel(x_ref, mod_ref, gpre_ref, gpost_ref, chan_ref, seq_ref, w_ref, o_ref, *, gd):
    x = x_ref[...]
    hb = _pre(x, mod_ref, gpre_ref).astype(BF16)
    u, v = _channel_dft(hb, chan_ref, gd)
    y = _dot(seq_ref[...], jnp.concatenate([u, v], axis=0)).astype(BF16)
    o_ref[...] = _post(x, _dot(y, w_ref[...]), mod_ref, gpost_ref, 1.0)


def _fourier_ctx(x, mod, gpre, gpost, w_out, lead):
    b, l, d = x.shape
    gd = d // FOURIER_GROUPS
    chan, _ = _fourier_consts(gd)
    c, s = _dft_cos_sin(l)
    seqm = jnp.asarray(np.concatenate([c, -s], axis=1), BF16)
    return pl.pallas_call(
        functools.partial(_fourier_ctx_kernel, gd=gd),
        grid=(b,),
        in_specs=[
            pl.BlockSpec((None, l, d), lambda i: (i, 0, 0)),
            pl.BlockSpec((None, 3, d), lambda i: (0, 0, 0)),
            _const_spec((1, d)),
            _const_spec((1, d)),
            _const_spec(chan.shape),
            _const_spec(seqm.shape),
            _pick_spec(w_out, lead),
        ],
        out_specs=pl.BlockSpec((None, l, d), lambda i: (i, 0, 0)),
        out_shape=jax.ShapeDtypeStruct(x.shape, F32),
        compiler_params=_params("parallel"),
        name="fourier_ctx",
    )(x, mod, gpre, gpost, chan, seqm, w_out)


def _lower_bound(logits_ref, j):
    lg = logits_ref[...]
    e = jnp.exp(lg - jnp.max(lg, axis=0, keepdims=True))
    p = e / jnp.sum(e, axis=0, keepdims=True)
    lb = jnp.zeros_like(p[0:1])
    for i in range(1, j + 1):
        lb = lb + p[i:i + 1]
    return lb


def _forget_gate(z, lb):
    e = jnp.exp(-jnp.abs(z))
    inv = 1.0 / (1.0 + e)
    pos = z >= 0.0
    sig = jnp.where(pos, inv, e * inv)
    nsig = jnp.where(pos, e * inv, inv)
    f = jnp.maximum(lb, LB_FLOOR) + (1.0 - lb) * sig
    return (1.0 - lb) * nsig, jnp.log2(f)


def _split2(x):
    hi = x.astype(BF16)
    lo = (x - hi.astype(F32)).astype(BF16)
    return hi, lo


def _tile_cumsum(g, tri_ref, tile, reverse):
    c = tri_ref.shape[0]
    parts = _split2(g)
    tri2 = jnp.concatenate([tri_ref[...]] * len(parts), axis=1)
    chunks = [_dot(tri2, jnp.concatenate([p[i:i + c] for p in parts], axis=0)) for i in range(0, g.shape[0], c)]
    per_tile = tile // c
    order = range(per_tile - 1, -1, -1) if reverse else range(per_tile)
    last = 0 if reverse else c - 1
    for t0 in range(0, len(chunks), per_tile):
        carry = None
        for i in order:
            if carry is not None:
                chunks[t0 + i] = chunks[t0 + i] + carry
            carry = chunks[t0 + i][last:last + 1, :]
    return jnp.concatenate(chunks, axis=0)


def _hgrn_in_kernel(x_ref, mod_ref, gpre_ref, w_ref, lbf_ref, lbb_ref, trif_ref, trib_ref,
                    q_ref, v_ref, kf_ref, bf_ref, kb_ref, bb_ref, sg_ref, *, layer_j, kd, hd, tile):
    hb = _pre(x_ref[...], mod_ref, gpre_ref).astype(BF16)
    col = lambda n: _dot(hb, w_ref[:, n * kd:(n + 1) * kd])
    q = _silu(col(0))
    v = col(1)
    kf, lf = _forget_gate(col(2), _lower_bound(lbf_ref, layer_j))
    kb, lbw = _forget_gate(col(3), _lower_bound(lbb_ref, layer_j))
    sg_ref[...] = _silu(col(4)).astype(BF16)
    bf = _tile_cumsum(lf, trif_ref, tile, False)
    bb = _tile_cumsum(lbw, trib_ref, tile, True)
    for h in range(HGRN_HEADS):
        sl = slice(h * hd, (h + 1) * hd)
        q_ref[h] = q[:, sl].astype(BF16)
        v_ref[h] = v[:, sl].astype(BF16)
        kf_ref[h] = kf[:, sl].astype(BF16)
        kb_ref[h] = kb[:, sl].astype(BF16)
        bf_ref[h] = bf[:, sl]
        bb_ref[h] = bb[:, sl]


def _hgrn_inputs(x, mod, gpre, w_in, lead, lb_fwd, lb_bwd, layer_j):
    b, l, d = x.shape
    kd = d
    hd = kd // HGRN_HEADS
    tm = _token_tile(l, MIXER_TOKENS)
    tile = _token_tile(tm, SCAN_TILE)
    t = _token_tile(tile, MXU_COLS)
    trif = _scan_consts(t, False)[0]
    trib = _scan_consts(t, True)[0]
    x_spec = pl.BlockSpec((None, tm, d), lambda i, j: (i, j, 0))
    head_spec = pl.BlockSpec((None, HGRN_HEADS, tm, hd), lambda i, j: (i, 0, j, 0))
    heads = lambda dt: jax.ShapeDtypeStruct((b, HGRN_HEADS, l, hd), dt)
    return pl.pallas_call(
        functools.partial(_hgrn_in_kernel, layer_j=layer_j, kd=kd, hd=hd, tile=tile),
        grid=(b, l // tm),
        in_specs=[
            x_spec,
            pl.BlockSpec((None, 3, d), lambda i, j: (i, 0, 0)),
            _const_spec((1, d)),
            _pick_spec(w_in, lead),
            _const_spec(lb_fwd.shape),
            _const_spec(lb_bwd.shape),
            _const_spec((t, t)),
            _const_spec((t, t)),
        ],
        out_specs=[head_spec] * 6 + [x_spec],
        out_shape=[heads(BF16), heads(BF16), heads(BF16), heads(F32), heads(BF16), heads(F32),
                   jax.ShapeDtypeStruct((b, l, d), BF16)],
        compiler_params=_params("parallel", "parallel"),
        name="hgrn_in",
    )(x, mod, gpre, w_in, lb_fwd, lb_bwd, trif, trib)


def _scan_consts(n, reverse):
    r = np.arange(n)[:, None]
    c = np.arange(n)[None, :]
    valid = (c >= r) if reverse else (c <= r)
    x = r ^ c
    lvl = np.where(x == 0, 0, np.floor(np.log2(np.maximum(x, 1))).astype(np.int64) + 1)
    fine = np.where(valid & (x < SUBLANES), lvl, -1)
    gap = np.where(valid, x, n)
    return jnp.asarray(valid, BF16), jnp.asarray(fine, jnp.int32), jnp.asarray(gap, jnp.int32)


def _level_ref(b, m, reverse):
    t, n = b.shape
    first = m if reverse else m - 1
    b3 = b.reshape(t // SUBLANES, SUBLANES, n)
    rid = lax.broadcasted_iota(jnp.int32, b3.shape, 1)
    ref = None
    for p in range(0, SUBLANES, 2 * m):
        row = jnp.broadcast_to(b3[:, p + first:p + first + 1, :], b3.shape)
        ref = row if ref is None else jnp.where(rid >= p, row, ref)
    return ref.reshape(t, n)


def _neg_abs(d):
    bits = lax.bitcast_convert_type(d, jnp.int32) | jnp.int32(-2 ** 31)
    return lax.bitcast_convert_type(bits, F32)


def _scaled(z, e):
    return (z * e).astype(BF16)


def _halves(lo, m, reverse):
    return (lo, lo + m, lo + m) if reverse else (lo + m, lo, lo + m - 1)


def _strip_scores(q, k, q32, k32, b, fine, gap, lane, reverse, fast):
    n = q.shape[0]
    if fast:
        mid = lambda i: 0.5 * (b[i:i + 1, :] + b[i + fast - 1:i + fast, :])
        ref = jnp.concatenate([jnp.broadcast_to(mid(i), (fast, b.shape[1])) for i in range(0, n, fast)], axis=0)
        d = b - ref
        a = jnp.where(gap < fast, _dot_nt(_scaled(q32, jnp.exp2(d)), _scaled(k32, jnp.exp2(-d))), 0.0)
        m = fast
    else:
        a = jnp.where(fine == 0, _dot_nt(q, k), 0.0)
        before = pltpu.roll(b, (n - 1) if reverse else 1, 0)
        a = jnp.where(fine == 1, _dot_nt(_scaled(q32, jnp.exp2(_neg_abs(b - before))), k), a)
        m, level = 2, 2
        while m < SUBLANES:
            e = jnp.exp2(_neg_abs(b - _level_ref(b, m, reverse)))
            a = jnp.where(fine == level, _dot_nt(_scaled(q32, e), _scaled(k32, e)), a)
            m, level = 2 * m, level + 1
    blocks = [a[i:i + SUBLANES] for i in range(0, n, SUBLANES)]
    while m < n:
        ql, kf = [], []
        for lo in range(0, n, 2 * m):
            late, early, first = _halves(lo, m, reverse)
            ref = b[first:first + 1, :]
            ql.append(_scaled(q32[late:late + m], jnp.exp2(b[late:late + m] - ref)))
            ke = _scaled(k32[early:early + m], jnp.exp2(ref - b[early:early + m]))
            kf += [k[late:late + m], ke] if reverse else [ke, k[late:late + m]]
        s_m = _dot_nt(jnp.concatenate(ql, axis=0), jnp.concatenate(kf, axis=0))
        for pi, lo in enumerate(range(0, n, 2 * m)):
            late, early, _ = _halves(lo, m, reverse)
            inside = (lane >= early) & (lane < early + m)
            for i in range(0, m, SUBLANES):
                rb = (late + i) // SUBLANES
                blocks[rb] = jnp.where(inside, s_m[pi * m + i:pi * m + i + SUBLANES], blocks[rb])
        m *= 2
    return jnp.concatenate(blocks, axis=0)


def _scan_tile(q, k, v, b, st, fine, gap, reverse, fast):
    t = q.shape[0]
    n = fine.shape[0]
    q32 = q.astype(F32)
    k32 = k.astype(F32)
    end = 0 if reverse else t - 1
    b_end = b[end:end + 1, :]

    o = _dot_nt(_scaled(q32, jnp.exp2(b)), st.astype(BF16))
    new_st = st * jnp.exp2(b_end) + _dot_tn(v, _scaled(k32, jnp.exp2(b_end - b)))

    lane = lax.broadcasted_iota(jnp.int32, (SUBLANES, n), 1)
    strips = range(0, t, n)
    score = {(i, i): _strip_scores(q[i:i + n], k[i:i + n], q32[i:i + n], k32[i:i + n], b[i:i + n],
                                   fine, gap, lane, reverse, fast) for i in strips}
    m = n
    while m < t:
        for lo in range(0, t, 2 * m):
            late, early, first = _halves(lo, m, reverse)
            ref = b[first:first + 1, :]
            ql = _scaled(q32[late:late + m], jnp.exp2(b[late:late + m] - ref))
            ke = _scaled(k32[early:early + m], jnp.exp2(ref - b[early:early + m]))
            s_m = _dot_nt(ql, ke)
            for i in range(0, m, n):
                for j in range(0, m, n):
                    score[(late + i, early + j)] = s_m[i:i + n, j:j + n]
        m *= 2
    outs = []
    for i in strips:
        keys = [j for j in strips if (i, j) in score]
        a = jnp.concatenate([score[(i, j)] for j in keys], axis=1).astype(BF16)
        outs.append(_dot(a, jnp.concatenate([v[j:j + n] for j in keys], axis=0)))
    return o + jnp.concatenate(outs, axis=0), new_st


def _scan_kernel(q_ref, k_ref, v_ref, b_ref, s0_ref, fine_ref, gap_ref, o_ref, sfin_ref, state_ref,
                 *, reverse, n_tiles):
    step = pl.program_id(2)

    @pl.when(step == 0)
    def _():
        state_ref[...] = s0_ref[...]

    nb, hp, t, _ = q_ref.shape

    def spread(blk):
        worst = None
        for bi in range(nb):
            for h in range(hp):
                top = b_ref[bi, h, pl.ds(0, t // blk, stride=blk), :]
                bottom = b_ref[bi, h, pl.ds(blk - 1, t // blk, stride=blk), :]
                s = jnp.abs(top - bottom)
                worst = s if worst is None else jnp.maximum(worst, s)
        return jnp.max(worst)

    def run(fast):
        fine = fine_ref[...]
        gap = gap_ref[...]

        def one_batch_element(bi, carry):
            for h in range(hp):
                o, st = _scan_tile(q_ref[bi, h], k_ref[bi, h], v_ref[bi, h], b_ref[bi, h], state_ref[bi, h],
                                   fine, gap, reverse, fast)
                state_ref[bi, h] = st
                o_ref[bi, h] = o.astype(o_ref.dtype)
            return carry

        lax.fori_loop(0, nb, one_batch_element, 0)

    blocks = [blk for blk in FAST_BLOCKS if t % blk == 0]
    fits = [spread(blk) <= FAST_RANGE for blk in blocks]
    unresolved = None
    for blk, ok in zip(blocks, fits):
        pl.when(ok if unresolved is None else unresolved & ok)(functools.partial(run, blk))
        unresolved = jnp.logical_not(ok) if unresolved is None else unresolved & jnp.logical_not(ok)
    if unresolved is None:
        run(0)
    else:
        pl.when(unresolved)(functools.partial(run, 0))

    @pl.when(step == n_tiles - 1)
    def _():
        sfin_ref[...] = state_ref[...]


def _scan(q, k, v, bcum, s0, reverse):
    b, h, l, hd = q.shape
    t = _token_tile(l, SCAN_TILE)
    n_tiles = l // t
    hp = SCAN_HEADS
    _, fine, gap = _scan_consts(LANES, reverse)
    tile = (lambda j: n_tiles - 1 - j) if reverse else (lambda j: j)
    nb = SCAN_BATCH if b % SCAN_BATCH == 0 else 1
    tok_spec = pl.BlockSpec((nb, hp, t, hd), lambda i, hh, j: (i, hh, tile(j), 0))
    st_spec = pl.BlockSpec((nb, hp, hd, hd), lambda i, hh, j: (i, hh, 0, 0))
    return pl.pallas_call(
        functools.partial(_scan_kernel, reverse=reverse, n_tiles=n_tiles),
        grid=(b // nb, h // hp, n_tiles),
        in_specs=[tok_spec, tok_spec, tok_spec, tok_spec, st_spec, _const_spec(fine.shape), _const_spec(gap.shape)],
        out_specs=[tok_spec, st_spec],
        out_shape=[jax.ShapeDtypeStruct((b, h, l, hd), BF16), jax.ShapeDtypeStruct((b, h, hd, hd), F32)],
        scratch_shapes=[pltpu.VMEM((nb, hp, hd, hd), F32)],
        compiler_params=_params("parallel", "parallel", "arbitrary"),
        name="scan_bwd" if reverse else "scan_fwd",
    )(q, k, v, bcum, s0, fine, gap)


def _hgrn_out_ffn_kernel(of_ref, ob_ref, sg_ref, x_ref, mod_ref, gpost_ref, gn_ref, w_ref,
                         fmod_ref, fgpre_ref, fgpost_ref, win_ref, wout_ref, o_ref):
    heads = []
    for h in range(HGRN_HEADS):
        heads.append(_rms(of_ref[h].astype(F32) + ob_ref[h].astype(F32), gn_ref[...]))
    o = jnp.concatenate(heads, axis=1)
    y = _dot((o * sg_ref[...].astype(F32)).astype(BF16), w_ref[...])
    x = _post(x_ref[...], y, mod_ref, gpost_ref, 1.0)
    o_ref[...] = _ffn_step(x, fmod_ref, fgpre_ref, fgpost_ref, win_ref, wout_ref)


def _hgrn_readout_ffn(o_f, o_b, sg, x, mod, gpost, g_norm, w_out, lead, fmod, fgpre, fgpost, w_in, w_ffn, flead):
    b, l, d = x.shape
    hd = d // HGRN_HEADS
    assert w_ffn.shape[-2] % MXU_COLS == 0
    tm = _token_tile(l, MIXER_TOKENS)
    x_spec = pl.BlockSpec((None, tm, d), lambda i, j: (i, j, 0))
    head_spec = pl.BlockSpec((None, HGRN_HEADS, tm, hd), lambda i, j: (i, 0, j, 0))
    mod_spec = pl.BlockSpec((None, 3, d), lambda i, j: (i, 0, 0))
    return pl.pallas_call(
        _hgrn_out_ffn_kernel,
        grid=(b, l // tm),
        in_specs=[
            head_spec, head_spec, x_spec, x_spec,
            mod_spec,
            _const_spec((1, d)),
            _const_spec((1, hd)),
            _pick_spec(w_out, lead),
            mod_spec,
            _const_spec((1, d)),
            _const_spec((1, d)),
            _pick_spec(w_in, flead),
            _pick_spec(w_ffn, flead),
        ],
        out_specs=x_spec,
        out_shape=jax.ShapeDtypeStruct(x.shape, F32),
        compiler_params=_params("parallel", "parallel"),
        name="hgrn_out_ffn",
    )(o_f, o_b, sg, x, mod, gpost, g_norm, w_out, fmod, fgpre, fgpost, w_in, w_ffn)


def kernel(x, c, ctx, c_ctx, ada_w, ada_b, norm_pre, norm_post, ffn_w_in, ffn_w_out, fourier_w_out,
           hgrn_w_in, hgrn_lb_fwd, hgrn_lb_bwd, hgrn_norm, hgrn_w_out):
    bsz, _, d = x.shape
    depth = ada_w.shape[0]
    assert bsz + 1 <= MOD_ROWS

    c_rows = jnp.concatenate([c, c_ctx[None, :], jnp.zeros((MOD_ROWS - bsz - 1, d), F32)], axis=0)
    mod = _adaln(c_rows, ada_w, ada_b).reshape(depth, MOD_ROWS, N_MOD, d)

    ffn_w_in = ffn_w_in.astype(BF16)
    ffn_w_out = ffn_w_out.astype(BF16)
    fourier_w_out = fourier_w_out.astype(BF16)
    hgrn_w_in = hgrn_w_in.astype(BF16)
    hgrn_w_out = hgrn_w_out.astype(BF16)

    for i in range(depth):
        last = i == depth - 1
        is_hgrn = i % 2 == 1
        jm = i // 2
        mx = lambda j: mod[i, :bsz, 3 * j:3 * j + 3]
        mc = lambda j: mod[i, bsz:bsz + 1, 3 * j:3 * j + 3]
        gpre = lambda j: norm_pre[i, j][None, :]
        gpost = lambda j: norm_post[i, j][None, :]
        ctx_mod = lambda j: jnp.broadcast_to(mc(j), (bsz, 3, d))
        ffn = lambda s, m, j, f: _ffn(s, m(j), gpre(j), gpost(j), ffn_w_in, ffn_w_out, (i, f))

        def ctx_ffn(s, j, f):
            n = s.shape[0] * s.shape[1]
            rows = FFN_TOKENS if n % FFN_TOKENS == 0 else s.shape[1]
            m = jnp.broadcast_to(mc(j), (n // rows, 3, d))
            return _ffn(s.reshape(n // rows, rows, d), m, gpre(j), gpost(j), ffn_w_in, ffn_w_out, (i, f)).reshape(s.shape)

        if not is_hgrn:
            x = _ffn_fourier_latent(x, (mx(0), mx(2)), (gpre(0), gpre(2)), (gpost(0), gpost(2)),
                                    ffn_w_in, ffn_w_out, (i, 0), (i, 1),
                                    mx(1), gpre(1), gpost(1), fourier_w_out, (jm,))
            if not last:
                ctx = ctx_ffn(ctx, 0, 0)
                ctx = _fourier_ctx(ctx, mc(1), gpre(1), gpost(1), fourier_w_out, (jm,))
                ctx = ctx_ffn(ctx, 2, 1)
        else:
            x = ffn(x, mx, 0, 0)
            ctx = ctx_ffn(ctx, 0, 0)
            hd = d // HGRN_HEADS
            zero = jnp.zeros((bsz, HGRN_HEADS, hd, hd), F32)
            qc, vc, kfc, bfc, kbc, bbc, sgc = _hgrn_inputs(
                ctx, ctx_mod(1), gpre(1), hgrn_w_in, (jm,), hgrn_lb_fwd, hgrn_lb_bwd, jm)
            ocf, s_f = _scan(qc, kfc, vc, bfc, zero, False)
            ocb, s_b = _scan(qc, kbc, vc, bbc, zero, True)
            qx, vx, kfx, bfx, kbx, bbx, sgx = _hgrn_inputs(
                x, mx(1), gpre(1), hgrn_w_in, (jm,), hgrn_lb_fwd, hgrn_lb_bwd, jm)
            oxf, _ = _scan(qx, kfx, vx, bfx, s_f, False)
            oxb, _ = _scan(qx, kbx, vx, bbx, s_b, True)
            gn = hgrn_norm[jm][None, :]
            x = _hgrn_readout_ffn(oxf, oxb, sgx, x, mx(1), gpost(1), gn, hgrn_w_out, (jm,),
                                  mx(2), gpre(2), gpost(2), ffn_w_in, ffn_w_out, (i, 1))
            if not last:
                ctx = _hgrn_readout_ffn(ocf, ocb, sgc, ctx, ctx_mod(1), gpost(1), gn, hgrn_w_out, (jm,),
                                        ctx_mod(2), gpre(2), gpost(2), ffn_w_in, ffn_w_out, (i, 1))
    return x
```
